```python
import jax, jax.numpy as jnp
from jax import lax
import numpy as np

D_MODEL = 2048
BATCH = 2
SEQ = 16384
DEPTH = 2
DEC_BATCH = 16
DEC_SEQ = 32
PAST_LEN = 4096

CHUNK = 64
Q_BLOCK = 128
EPS = 1e-6

MLA_HEADS = 8
Q_LORA = 512
KV_LORA = 256
NOPE_DIM = 128
ROPE_DIM = 64
V_HEAD = 128
ROPE_THETA = 10000.0
MLA_SCALE = (NOPE_DIM + ROPE_DIM) ** -0.5

GM_GROUPS = 8
GM_GROUP_DIM = 128
GM_WIDTH = GM_GROUPS * GM_GROUP_DIM
GM_CHUNK = 128

EVEN_IN_SPLITS = (Q_LORA, KV_LORA, ROPE_DIM, GM_WIDTH, GM_WIDTH)
EVEN_MIX_WIDTH = MLA_HEADS * V_HEAD + GM_WIDTH

GLA_HEADS = 4
GLA_DK = D_MODEL // 2 // GLA_HEADS
GLA_DV = D_MODEL // GLA_HEADS
GLA_GATE_RANK = 16
GLA_TAU = 16.0
ODD_IN_SPLITS = (GLA_HEADS * GLA_DK, GLA_HEADS * GLA_DK, GLA_HEADS * GLA_DV, GLA_HEADS * GLA_DV, GLA_GATE_RANK)

N_GROUPS = 8
EXPERTS_PER_GROUP = 8
N_EXPERTS = N_GROUPS * EXPERTS_PER_GROUP
TOP_K = 2
D_EXPERT = 512
MOE_BLOCK = 128

N_MLA_LAYERS = (DEPTH + 1) // 2
N_GLA_LAYERS = DEPTH // 2

kernel_name = 'hybrid_streaming_mla_gmlp_gla_hmoe_step'


def _split(x, widths):
    offsets = [int(o) for o in np.cumsum(widths)[:-1]]
    return jnp.split(x, offsets, axis=-1)


def rms_norm(x, g):
    xf = x.astype(jnp.float32)
    y = xf * lax.rsqrt(jnp.mean(xf * xf, axis=-1, keepdims=True) + EPS)
    return (y * g.astype(jnp.float32)).astype(x.dtype)


def layer_norm(x, g, b):
    xf = x.astype(jnp.float32)
    xc = xf - jnp.mean(xf, axis=-1, keepdims=True)
    y = xc * lax.rsqrt(jnp.mean(xc * xc, axis=-1, keepdims=True) + EPS)
    return (y * g.astype(jnp.float32) + b.astype(jnp.float32)).astype(x.dtype)


def adaln(c, w, b):
    m = jax.nn.silu(c) @ w + b
    shift, scale, gate = jnp.split(m, 3, axis=-1)
    return shift[:, None, :], scale[:, None, :], gate[:, None, :]


def modulate(h, shift, scale):
    return h * (1.0 + scale) + shift


def apply_rope(x, pos):
    half = ROPE_DIM // 2
    inv = ROPE_THETA ** (-jnp.arange(half, dtype=jnp.float32) / half)
    ang = pos.astype(jnp.float32)[:, None] * inv[None, :]
    shape = (ang.shape[0],) + (1,) * (x.ndim - 3) + (half,)
    cos = jnp.cos(ang).reshape(shape)
    sin = jnp.sin(ang).reshape(shape)
    xf = x.astype(jnp.float32)
    x1, x2 = xf[..., :half], xf[..., half:]
    return jnp.concatenate([x1 * cos - x2 * sin, x1 * sin + x2 * cos], axis=-1).astype(x.dtype)


def chunk_causal_latent_attention(q_lat, q_rope, ckv, krope, q_pos, k_pos):
    B, T, H, C = q_lat.shape
    qb = min(Q_BLOCK, T)
    nb = T // qb
    k_chunk = k_pos // CHUNK

    def one_block(args):
        ql, qr, qp = args
        s = (jnp.einsum('bqhc,bkc->bhqk', ql, ckv) + jnp.einsum('bqhr,bkr->bhqk', qr, krope)).astype(jnp.float32) * MLA_SCALE
        allowed = k_chunk[None, :] <= (qp // CHUNK)[:, None]
        p = jax.nn.softmax(jnp.where(allowed, s, -jnp.inf), axis=-1).astype(ckv.dtype)
        return jnp.einsum('bhqk,bkc->bqhc', p, ckv)

    blocks = (q_lat.reshape(B, nb, qb, H, C).swapaxes(0, 1),
              q_rope.reshape(B, nb, qb, H, ROPE_DIM).swapaxes(0, 1),
              q_pos.reshape(nb, qb))
    out = lax.map(one_block, blocks)
    return out.swapaxes(0, 1).reshape(B, T, H, C)


def gmlp_spatial(u, v, ln_g, ln_b, ws, bs):
    B, T, _ = u.shape
    u = jax.nn.gelu(u)
    v = layer_norm(jax.nn.gelu(v), ln_g, ln_b)
    n = -(-T // GM_CHUNK)
    vp = jnp.pad(v, ((0, 0), (0, n * GM_CHUNK - T), (0, 0))).reshape(B, n, GM_CHUNK, GM_GROUPS, GM_GROUP_DIM)
    w = ws * jnp.tril(jnp.ones((GM_CHUNK, GM_CHUNK), ws.dtype))
    mixed = jnp.einsum('gij,bnjgc->bnigc', w, vp) + bs.T[:, :, None]
    mixed = mixed.reshape(B, n * GM_CHUNK, GM_WIDTH)[:, :T]
    return u * mixed, v


def mla_gmlp_mixer(h, pos0, past_ckv, past_krope, w_in, q_norm, w_uq, kv_norm, w_uk, w_uv,
                   ln_g, ln_b, ws, bs, w_out):
    B, T, _ = h.shape
    c_q, c_kv, k_r, u, v = _split(h @ w_in, EVEN_IN_SPLITS)
    pos = pos0 + jnp.arange(T, dtype=jnp.int32)
    q = (rms_norm(c_q, q_norm) @ w_uq).reshape(B, T, MLA_HEADS, NOPE_DIM + ROPE_DIM)
    q_nope = q[..., :NOPE_DIM]
    q_rope = apply_rope(q[..., NOPE_DIM:], pos)
    ckv = rms_norm(c_kv, kv_norm)
    krope = apply_rope(k_r, pos)
    q_lat = jnp.einsum('bthn,chn->bthc', q_nope, w_uk)
    if past_ckv is None:
        keys_ckv, keys_krope, k_pos = ckv, krope, pos
    else:
        keys_ckv = jnp.concatenate([past_ckv.astype(ckv.dtype), ckv], axis=1)
        keys_krope = jnp.concatenate([past_krope.astype(krope.dtype), krope], axis=1)
        k_pos = jnp.arange(past_ckv.shape[1] + T, dtype=jnp.int32)
    o_lat = chunk_causal_latent_attention(q_lat, q_rope, keys_ckv, keys_krope, pos, k_pos)
    attn = jnp.einsum('bthc,chd->bthd', o_lat, w_uv).reshape(B, T, MLA_HEADS * V_HEAD)
    gm, v_rows = gmlp_spatial(u, v, ln_g, ln_b, ws, bs)
    y = jnp.concatenate([attn, gm], axis=-1) @ w_out
    return y, ckv, krope, v_rows


def gla_chunk_scan(q, k, v, g, s0):
    B, T, H, _ = q.shape
    n = -(-T // CHUNK)
    pad = n * CHUNK - T

    def to_chunks(a):
        a = jnp.pad(a, ((0, 0), (0, pad), (0, 0), (0, 0)))
        return a.reshape(B, n, CHUNK, H, a.shape[-1]).transpose(1, 0, 3, 2, 4)

    tri = jnp.tril(jnp.ones((CHUNK, CHUNK), bool))

    def step(S, inp):
        qc, kc, vc, gc = inp
        cum = jnp.cumsum(gc, axis=2)
        o_inter = jnp.einsum('bhck,bhkv->bhcv', qc * jnp.exp(cum), S)
        diff = cum[:, :, :, None, :] - cum[:, :, None, :, :]
        decay = jnp.where(tri[:, :, None], jnp.exp(jnp.minimum(diff, 0.0)), 0.0)
        att = jnp.einsum('bhik,bhjk,bhijk->bhij', qc, kc, decay)
        o = o_inter + jnp.einsum('bhij,bhjv->bhiv', att, vc)
        last = cum[:, :, -1:, :]
        S = jnp.exp(last[:, :, 0, :])[..., None] * S + jnp.einsum('bhck,bhcv->bhkv', kc * jnp.exp(last - cum), vc)
        return S, o

    S, o = lax.scan(step, s0, (to_chunks(q), to_chunks(k), to_chunks(v), to_chunks(g)))
    o = o.transpose(1, 0, 3, 2, 4).reshape(B, n * CHUNK, H, v.shape[-1])[:, :T]
    return o, S


def gla_mixer(h, s0, w_in, w_gate2, b_gate, head_norm, w_out):
    B, T, _ = h.shape
    q, k, v, r, g_low = _split(h @ w_in, ODD_IN_SPLITS)
    log_a = jax.nn.log_sigmoid((g_low @ w_gate2 + b_gate).astype(jnp.float32)) / GLA_TAU

    def heads(a, d):
        return a.reshape(B, T, GLA_HEADS, d).astype(jnp.float32)

    o, s = gla_chunk_scan(heads(q, GLA_DK) * GLA_DK ** -0.5, heads(k, GLA_DK), heads(v, GLA_DV),
                          heads(log_a, GLA_DK), s0)
    o = rms_norm(o, head_norm).astype(h.dtype).reshape(B, T, GLA_HEADS * GLA_DV)
    return (o * jax.nn.silu(r)) @ w_out, s


def expert_dispatch(x, expert_idx, expert_wt, w_gate, w_up, w_down):
    T, D = x.shape
    A = T * TOP_K
    flat_e = expert_idx.reshape(A)
    flat_w = expert_wt.reshape(A)
    order = jnp.argsort(flat_e)
    se = flat_e[order]
    counts = jnp.bincount(flat_e, length=N_EXPERTS)
    padded = (counts + MOE_BLOCK - 1) // MOE_BLOCK * MOE_BLOCK
    pad_end = jnp.cumsum(padded)
    pad_start = pad_end - padded
    start = jnp.cumsum(counts) - counts
    dest = pad_start[se] + jnp.arange(A) - start[se]
    n_blocks = -(-A // MOE_BLOCK) + N_EXPERTS
    n_slots = n_blocks * MOE_BLOCK
    slot_tok = jnp.zeros((n_slots,), jnp.int32).at[dest].set((order // TOP_K).astype(jnp.int32))
    slot_wt = jnp.zeros((n_slots,), jnp.float32).at[dest].set(flat_w[order])
    block_e = jnp.minimum(jnp.searchsorted(pad_end, jnp.arange(n_blocks) * MOE_BLOCK, side='right'), N_EXPERTS - 1)
    xb = x[slot_tok].reshape(n_blocks, MOE_BLOCK, D)

    def run_block(args):
        xblk, e = args
        hid = jax.nn.silu(xblk @ w_gate[e]) * (xblk @ w_up[e])
        return hid @ w_down[e]

    yb = lax.map(run_block, (xb, block_e)).reshape(n_slots, D)
    out = jnp.zeros((T, D), jnp.float32).at[slot_tok].add(yb.astype(jnp.float32) * slot_wt[:, None])
    return out.astype(x.dtype)


def hmoe(x, wg, bg, we, be, w_gate, w_up, w_down):
    T = x.shape[0]
    g_logits = (x @ wg).astype(jnp.float32) + bg.astype(jnp.float32)
    grp = jnp.argmax(g_logits, axis=-1)
    p_grp = jnp.max(jax.nn.softmax(g_logits, axis=-1), axis=-1)
    e_all = ((x @ we).astype(jnp.float32) + be.astype(jnp.float32)).reshape(T, N_GROUPS, EXPERTS_PER_GROUP)
    e_logits = jnp.sum(e_all * jax.nn.one_hot(grp, N_GROUPS, dtype=jnp.float32)[:, :, None], axis=1)
    top_v, top_i = lax.top_k(e_logits, TOP_K)
    top_w = jax.nn.softmax(top_v, axis=-1) * p_grp[:, None]
    expert_idx = grp[:, None].astype(jnp.int32) * EXPERTS_PER_GROUP + top_i.astype(jnp.int32)
    return expert_dispatch(x, expert_idx, top_w, w_gate, w_up, w_down)


def setup_inputs(seed: int = 0) -> dict:
    key = jax.random.key(seed)
    keys = iter(jax.random.split(key, 48))
    D = D_MODEL
    nm, ng = N_MLA_LAYERS, N_GLA_LAYERS

    def normal(shape, scale):
        return jax.random.normal(next(keys), shape, jnp.float32) * scale

    def gain(shape):
        return 1.0 + 0.05 * jax.random.normal(next(keys), shape, jnp.float32)

    return {
        'x_prompt': normal((BATCH, SEQ, D), 1.0),
        'x_sample': normal((DEC_BATCH, DEC_SEQ, D), 1.0),
        'cache_ckv': normal((nm, DEC_BATCH, PAST_LEN, KV_LORA), 1.0),
        'cache_krope': normal((nm, DEC_BATCH, PAST_LEN, ROPE_DIM), 1.0),
        'state_gla': normal((ng, DEC_BATCH, GLA_HEADS, GLA_DK, GLA_DV), 0.5),
        'c_prompt': normal((BATCH, D), 1.0),
        'c_sample': normal((DEC_BATCH, D), 1.0),
        'ada_w': normal((DEPTH, 2, D, 3 * D), D ** -0.5),
        'ada_b': normal((DEPTH, 2, 3 * D), 0.02),
        'norm_mix': gain((DEPTH, D)),
        'norm_ffn': gain((DEPTH, D)),
        'mla_w_in': normal((nm, D, sum(EVEN_IN_SPLITS)), D ** -0.5),
        'mla_q_norm': gain((nm, Q_LORA)),
        'mla_w_uq': normal((nm, Q_LORA, MLA_HEADS * (NOPE_DIM + ROPE_DIM)), Q_LORA ** -0.5),
        'mla_kv_norm': gain((nm, KV_LORA)),
        'mla_w_uk': normal((nm, KV_LORA, MLA_HEADS, NOPE_DIM), KV_LORA ** -0.5),
        'mla_w_uv': normal((nm, KV_LORA, MLA_HEADS, V_HEAD), KV_LORA ** -0.5),
        'gm_ln_g': gain((nm, GM_WIDTH)),
        'gm_ln_b': normal((nm, GM_WIDTH), 0.02),
        'gm_ws': normal((nm, GM_GROUPS, GM_CHUNK, GM_CHUNK), GM_CHUNK ** -0.5),
        'gm_bs': gain((nm, GM_GROUPS, GM_CHUNK)),
        'mix_a_w_out': normal((nm, EVEN_MIX_WIDTH, D), EVEN_MIX_WIDTH ** -0.5),
        'gla_w_in': normal((ng, D, sum(ODD_IN_SPLITS)), D ** -0.5),
        'gla_w_gate2': normal((ng, GLA_GATE_RANK, GLA_HEADS * GLA_DK), GLA_GATE_RANK ** -0.5),
        'gla_b_gate': normal((ng, GLA_HEADS * GLA_DK), 0.1),
        'gla_norm': gain((ng, GLA_DV)),
        'gla_w_out': normal((ng, GLA_HEADS * GLA_DV, D), (GLA_HEADS * GLA_DV) ** -0.5),
        'moe_wg': normal((DEPTH, D, N_GROUPS), D ** -0.5),
        'moe_bg': normal((DEPTH, N_GROUPS), 0.01),
        'moe_we': normal((DEPTH, D, N_EXPERTS), D ** -0.5),
        'moe_be': normal((DEPTH, N_EXPERTS), 0.01),
        'moe_w_gate': normal((DEPTH, N_EXPERTS, D, D_EXPERT), D ** -0.5),
        'moe_w_up': normal((DEPTH, N_EXPERTS, D, D_EXPERT), D ** -0.5),
        'moe_w_down': normal((DEPTH, N_EXPERTS, D_EXPERT, D), D_EXPERT ** -0.5),
        'final_norm': gain((D,)),
    }


def reference(x_prompt, x_sample, cache_ckv, cache_krope, state_gla, c_prompt, c_sample,
              ada_w, ada_b, norm_mix, norm_ffn,
              mla_w_in, mla_q_norm, mla_w_uq, mla_kv_norm, mla_w_uk, mla_w_uv,
              gm_ln_g, gm_ln_b, gm_ws, gm_bs, mix_a_w_out,
              gla_w_in, gla_w_gate2, gla_b_gate, gla_norm, gla_w_out,
              moe_wg, moe_bg, moe_we, moe_be, moe_w_gate, moe_w_up, moe_w_down,
              final_norm):

    def trunk(x, c, pos0, past_ckv, past_krope, past_gla):
        B, T, _ = x.shape
        new_ckv, new_krope, new_gmv, new_gla = [], [], [], []
        for layer in range(DEPTH):
            i = layer // 2
            shift, scale, gate = adaln(c, ada_w[layer, 0], ada_b[layer, 0])
            h = modulate(rms_norm(x, norm_mix[layer]), shift, scale)
            if layer % 2 == 0:
                y, ckv, krope, gv = mla_gmlp_mixer(
                    h, pos0,
                    None if past_ckv is None else past_ckv[i],
                    None if past_krope is None else past_krope[i],
                    mla_w_in[i], mla_q_norm[i], mla_w_uq[i], mla_kv_norm[i], mla_w_uk[i], mla_w_uv[i],
                    gm_ln_g[i], gm_ln_b[i], gm_ws[i], gm_bs[i], mix_a_w_out[i])
                new_ckv.append(ckv)
                new_krope.append(krope)
                new_gmv.append(gv)
            else:
                if past_gla is None:
                    s0 = jnp.zeros((B, GLA_HEADS, GLA_DK, GLA_DV), jnp.float32)
                else:
                    s0 = past_gla[i].astype(jnp.float32)
                y, s = gla_mixer(h, s0, gla_w_in[i], gla_w_gate2[i], gla_b_gate[i], gla_norm[i], gla_w_out[i])
                new_gla.append(s)
            x = x + gate * y
            shift, scale, gate = adaln(c, ada_w[layer, 1], ada_b[layer, 1])
            h = modulate(rms_norm(x, norm_ffn[layer]), shift, scale)
            y = hmoe(h.reshape(B * T, D_MODEL), moe_wg[layer], moe_bg[layer], moe_we[layer], moe_be[layer],
                     moe_w_gate[layer], moe_w_up[layer], moe_w_down[layer]).reshape(B, T, D_MODEL)
            x = x + gate * y
        return (rms_norm(x, final_norm), jnp.stack(new_ckv), jnp.stack(new_krope),
                jnp.stack(new_gmv), jnp.stack(new_gla))

    y_prompt, ckv_p, krope_p, _gmv_p, gla_p = trunk(x_prompt, c_prompt, 0, None, None, None)
    y_sample, ckv_s, krope_s, gmv_s, gla_s = trunk(x_sample, c_sample, cache_ckv.shape[2],
                                                   cache_ckv, cache_krope, state_gla)
    return (y_prompt, y_sample, ckv_p, krope_p, ckv_s, krope_s, gmv_s, gla_p, gla_s)
```

```python
import functools
import math

import jax
import jax.numpy as jnp
from jax import lax
from jax.experimental import pallas as pl
from jax.experimental.pallas import tpu as pltpu

F32 = jnp.float32
BF16 = jnp.bfloat16

CHUNK = 64
EPS = 1e-6
MLA_HEADS = 8
Q_LORA = 512
KV_LORA = 256
NOPE_DIM = 128
ROPE_DIM = 64
V_HEAD = 128
ROPE_THETA = 10000.0
MLA_SCALE = (NOPE_DIM + ROPE_DIM) ** -0.5
QK_WIDTH = KV_LORA + 2 * ROPE_DIM
GM_GROUPS = 8
GM_GROUP_DIM = 128
GM_WIDTH = GM_GROUPS * GM_GROUP_DIM
GM_CHUNK = 128
GLA_HEADS = 4
GLA_GATE_RANK = 16
GLA_TAU = 16.0
N_GROUPS = 8
EXPERTS_PER_GROUP = 8
N_EXPERTS = N_GROUPS * EXPERTS_PER_GROUP
TOP_K = 2
LOG2E = 1.4426950408889634

LANES = 128
SUBLANES = 8
MOE_ROWS = 256
ATT_TK = 256
VMEM_LIMIT = 56 * 1024 * 1024


def _cparams(sem):
    return pltpu.CompilerParams(dimension_semantics=sem, vmem_limit_bytes=VMEM_LIMIT)


def _const_spec(shape):
    nd = len(shape)
    return pl.BlockSpec(shape, lambda *_: (0,) * nd, pipeline_mode=pl.Buffered(1))


def _dot(a, b):
    return jnp.dot(a, b, preferred_element_type=F32)


def _dot_nt(a, b):
    return lax.dot_general(a, b, (((1,), (1,)), ((), ())), preferred_element_type=F32)


def _dot_tn(a, b):
    return lax.dot_general(a, b, (((0,), (0,)), ((), ())), preferred_element_type=F32)


def _rms(x, g):
    return x * lax.rsqrt(jnp.mean(x * x, axis=-1, keepdims=True) + EPS) * g


def _silu(x):
    return x * (1.0 / (1.0 + jnp.exp(-x)))


def _ada_kernel(c_ref, w_ref, b_ref, o_ref):
    s = _silu(c_ref[...]).astype(BF16)
    o_ref[0, 0] = _dot(s, w_ref[0, 0].astype(BF16)) + b_ref[0, 0]


def _ada_call(c_all, ada_w, ada_b):
    depth, _, d, d3 = ada_w.shape
    r = c_all.shape[0]
    tn = 1024
    return pl.pallas_call(
        _ada_kernel,
        grid=(depth, 2, d3 // tn),
        in_specs=[
            pl.BlockSpec((r, d), lambda l, s, j: (0, 0)),
            pl.BlockSpec((1, 1, d, tn), lambda l, s, j: (l, s, 0, j)),
            pl.BlockSpec((1, 1, 1, tn), lambda l, s, j: (l, s, 0, j)),
        ],
        out_specs=pl.BlockSpec((1, 1, r, tn), lambda l, s, j: (l, s, 0, j)),
        out_shape=jax.ShapeDtypeStruct((depth, 2, r, d3), F32),
        compiler_params=_cparams(("arbitrary",) * 3),
        name="adaln",
    )(c_all, ada_w, ada_b.reshape(depth, 2, 1, d3))


def _pre0_kernel(x_ref, shift_ref, scale_ref, gmix_ref, win_ref, qn_ref, wuq_ref, kvn_ref, wuk_ref,
                 cs_ref, lng_ref, lnb_ref, ws_ref, bst_ref,
                 q_ref, kp_ref, ckv_ref, kr_ref, gm_ref, *rest, cl):
    tm = x_ref.shape[1]
    x = x_ref[0]
    h = (_rms(x, gmix_ref[...]) * (1.0 + scale_ref[0]) + shift_ref[0]).astype(BF16)
    y = _dot(h, win_ref[...])
    cs = cs_ref[...]

    cq = _rms(y[:, :Q_LORA], qn_ref[...]).astype(BF16)
    qf = _dot(cq, wuq_ref[...])
    qscale = MLA_SCALE * LOG2E
    nope_w = MLA_HEADS * NOPE_DIM
    for hd in range(MLA_HEADS):
        qn = qf[:, hd * NOPE_DIM:(hd + 1) * NOPE_DIM].astype(BF16)
        ql = _dot(qn, wuk_ref[hd])
        t = qf[:, nope_w + hd * LANES:nope_w + (hd + 1) * LANES] * cs
        qr = t + pltpu.roll(t, ROPE_DIM, 1)
        q_ref[0, hd, :, :KV_LORA] = (ql * qscale).astype(BF16)
        q_ref[0, hd, :, KV_LORA:] = (qr * qscale).astype(BF16)

    o0 = Q_LORA
    ckv = _rms(y[:, o0:o0 + KV_LORA], kvn_ref[...])
    ckv_ref[0] = ckv
    o1 = o0 + KV_LORA
    tk = y[:, o1:o1 + LANES] * cs
    kk = tk + pltpu.roll(tk, ROPE_DIM, 1)
    kr_ref[0] = kk[:, :ROPE_DIM]
    lane = lax.broadcasted_iota(jnp.int32, (tm, LANES), 1)
    kp_ref[0, :, :KV_LORA] = ckv.astype(BF16)
    kp_ref[0, :, KV_LORA:] = jnp.where(lane < ROPE_DIM, kk, 0.0).astype(BF16)

    o2 = o1 + LANES
    u = jax.nn.gelu(y[:, o2:o2 + GM_WIDTH])
    v = jax.nn.gelu(y[:, o2 + GM_WIDTH:o2 + 2 * GM_WIDTH])
    vc = v - jnp.mean(v, axis=-1, keepdims=True)
    v = vc * lax.rsqrt(jnp.mean(vc * vc, axis=-1, keepdims=True) + EPS) * lng_ref[...] + lnb_ref[...]
    if rest:
        rest[0][0] = v
    vb = v.astype(BF16)
    ri = lax.broadcasted_iota(jnp.int32, (cl, cl), 0)
    ci = lax.broadcasted_iota(jnp.int32, (cl, cl), 1)
    for g in range(GM_GROUPS):
        w = jnp.where(ci <= ri, ws_ref[g], 0.0).astype(BF16)
        bias = bst_ref[:, g:g + 1]
        gs = slice(g * GM_GROUP_DIM, (g + 1) * GM_GROUP_DIM)
        for c in range(tm // cl):
            rs = slice(c * cl, (c + 1) * cl)
            mixed = _dot(w, vb[rs, gs]) + bias
            gm_ref[0, rs, gs] = (u[rs, gs] * mixed).astype(BF16)


def _pre0_call(x, shift, scale, gmix, win, qn, wuq, kvn, wuk, cs, lng, lnb, ws, bst, *, tm, want_v):
    b, t, d = x.shape
    cl = ws.shape[1]
    nt = t // tm
    tok = lambda width: pl.BlockSpec((1, tm, width), lambda i, j: (i, j, 0))
    mod = pl.BlockSpec((1, 1, d), lambda i, j: (i, 0, 0))
    out_shape = [
        jax.ShapeDtypeStruct((b, MLA_HEADS, t, QK_WIDTH), BF16),
        jax.ShapeDtypeStruct((b, t, QK_WIDTH), BF16),
        jax.ShapeDtypeStruct((b, t, KV_LORA), F32),
        jax.ShapeDtypeStruct((b, t, ROPE_DIM), F32),
        jax.ShapeDtypeStruct((b, t, GM_WIDTH), BF16),
    ]
    out_specs = [
        pl.BlockSpec((1, MLA_HEADS, tm, QK_WIDTH), lambda i, j: (i, 0, j, 0)),
        tok(QK_WIDTH), tok(KV_LORA), tok(ROPE_DIM), tok(GM_WIDTH),
    ]
    if want_v:
        out_shape.append(jax.ShapeDtypeStruct((b, t, GM_WIDTH), F32))
        out_specs.append(tok(GM_WIDTH))
    return pl.pallas_call(
        functools.partial(_pre0_kernel, cl=cl),
        grid=(b, nt),
        in_specs=[
            tok(d), mod, mod, _const_spec(gmix.shape), _const_spec(win.shape), _const_spec(qn.shape),
            _const_spec(wuq.shape), _const_spec(kvn.shape), _const_spec(wuk.shape),
            pl.BlockSpec((tm, LANES), lambda i, j: (j, 0)),
            _const_spec(lng.shape), _const_spec(lnb.shape), _const_spec(ws.shape), _const_spec(bst.shape),
        ],
        out_specs=out_specs,
        out_shape=out_shape,
        compiler_params=_cparams(("arbitrary", "arbitrary")),
        name="mla_gmlp_pre",
    )(x, shift, scale, gmix, win, qn, wuq, kvn, wuk, cs, lng, lnb, ws, bst)


def _flash_kernel(q_ref, kv_ref, o_ref, m_ref, l_ref, acc_ref, *, tq, q_pos0, n_keys):
    qi = pl.program_id(1)
    rows = MLA_HEADS * tq
    q = q_ref[0].reshape(rows, QK_WIDTH)
    m_ref[...] = jnp.full((rows, 1), -jnp.inf, F32)
    l_ref[...] = jnp.zeros((rows, 1), F32)
    acc_ref[...] = jnp.zeros((rows, KV_LORA), F32)
    q_start = q_pos0 + qi * tq
    n_full = q_start // ATT_TK
    last_chunk_end = ((q_start + tq - 1) // CHUNK + 1) * CHUNK
    n_need = (jnp.minimum(last_chunk_end, n_keys) + ATT_TK - 1) // ATT_TK

    def step(j, masked):
        k = kv_ref[0, pl.ds(pl.multiple_of(j * ATT_TK, ATT_TK), ATT_TK), :]
        s = _dot_nt(q, k)
        if masked:
            row = lax.broadcasted_iota(jnp.int32, (rows, ATT_TK), 0)
            col = lax.broadcasted_iota(jnp.int32, (rows, ATT_TK), 1)
            qpos = q_start + (row & (tq - 1))
            kpos = j * ATT_TK + col
            vis = (kpos < n_keys) & ((kpos // CHUNK) <= (qpos // CHUNK))
            s = jnp.where(vis, s, -jnp.inf)
        m_prev = m_ref[...]
        m_new = jnp.maximum(m_prev, jnp.max(s, axis=-1, keepdims=True))
        alpha = jnp.exp2(m_prev - m_new)
        p = jnp.exp2(s - m_new)
        l_ref[...] = alpha * l_ref[...] + jnp.sum(p, axis=-1, keepdims=True)
        acc_ref[...] = alpha * acc_ref[...] + _dot(p.astype(BF16), k[:, :KV_LORA])
        m_ref[...] = m_new

    def full_body(j, c):
        step(j, False)
        return c

    def masked_body(j, c):
        step(j, True)
        return c

    lax.fori_loop(0, n_full, full_body, 0)
    lax.fori_loop(n_full, n_need, masked_body, 0)
    o = acc_ref[...] * (1.0 / l_ref[...])
    o_ref[0] = o.reshape(MLA_HEADS, tq, KV_LORA).astype(BF16)


def _flash_call(qp, kp, *, tq, q_pos0, n_keys):
    b, hh, t, _ = qp.shape
    tk_total = kp.shape[1]
    assert tq & (tq - 1) == 0 and tk_total % ATT_TK == 0
    rows = hh * tq
    return pl.pallas_call(
        functools.partial(_flash_kernel, tq=tq, q_pos0=q_pos0, n_keys=n_keys),
        grid=(b, t // tq),
        in_specs=[
            pl.BlockSpec((1, hh, tq, QK_WIDTH), lambda i, j: (i, 0, j, 0)),
            pl.BlockSpec((1, tk_total, QK_WIDTH), lambda i, j: (i, 0, 0), pipeline_mode=pl.Buffered(1)),
        ],
        out_specs=pl.BlockSpec((1, hh, tq, KV_LORA), lambda i, j: (i, 0, j, 0)),
        out_shape=jax.ShapeDtypeStruct((b, hh, t, KV_LORA), BF16),
        scratch_shapes=[pltpu.VMEM((rows, 1), F32), pltpu.VMEM((rows, 1), F32),
                        pltpu.VMEM((rows, KV_LORA), F32)],
        compiler_params=_cparams(("arbitrary", "arbitrary")),
        name="mla_flash",
    )(qp, kp)


def _moe_pre(x1, nf_ref, shift_ref, scale_ref, wrh_ref, wrl_ref, rb_ref, x1_ref, h2_ref, route_ref):
    tm = x1.shape[0]
    x1_ref[0] = x1
    h2 = _rms(x1, nf_ref[...]) * (1.0 + scale_ref[0]) + shift_ref[0]
    h_hi = h2.astype(BF16)
    h_lo = (h2 - h_hi.astype(F32)).astype(BF16)
    h2_ref[0] = h2
    logits = _dot(h_hi, wrh_ref[...]) + _dot(h_hi, wrl_ref[...]) + _dot(h_lo, wrh_ref[...]) + rb_ref[...]
    lane = lax.broadcasted_iota(jnp.int32, (tm, LANES), 1).astype(F32)
    neg = -jnp.inf
    is_g = lane < N_GROUPS
    gmax = jnp.max(jnp.where(is_g, logits, neg), axis=-1, keepdims=True)
    grp = jnp.min(jnp.where(is_g & (logits == gmax), lane, float(LANES)), axis=-1, keepdims=True)
    p_grp = 1.0 / jnp.sum(jnp.where(is_g, jnp.exp(logits - gmax), 0.0), axis=-1, keepdims=True)
    lo = N_GROUPS + EXPERTS_PER_GROUP * grp
    is_e = (lane >= lo) & (lane < lo + EXPERTS_PER_GROUP)
    el = jnp.where(is_e, logits, neg)
    v1 = jnp.max(el, axis=-1, keepdims=True)
    i1 = jnp.min(jnp.where(is_e & (logits == v1), lane, float(LANES)), axis=-1, keepdims=True)
    el2 = jnp.where(lane == i1, neg, el)
    v2 = jnp.max(el2, axis=-1, keepdims=True)
    i2 = jnp.min(jnp.where(is_e & (lane != i1) & (logits == v2), lane, float(LANES)), axis=-1, keepdims=True)
    e = jnp.exp(v2 - v1)
    w1 = p_grp / (1.0 + e)
    w2 = p_grp * e / (1.0 + e)
    route = jnp.where(lane == 0.0, i1 - N_GROUPS,
                      jnp.where(lane == 1.0, i2 - N_GROUPS,
                                jnp.where(lane == 2.0, w1, jnp.where(lane == 3.0, w2, 0.0))))
    route_ref[0] = route


def _moe_pre_specs(d, tm):
    mod = pl.BlockSpec((1, 1, d), lambda i, j: (i, 0, 0))
    return [_const_spec((1, d)), mod, mod, _const_spec((d, LANES)), _const_spec((d, LANES)),
            _const_spec((1, LANES))]


def _moe_pre_outs(b, t, d, tm):
    tok = lambda width: pl.BlockSpec((1, tm, width), lambda i, j: (i, j, 0))
    shapes = [jax.ShapeDtypeStruct((b, t, d), F32), jax.ShapeDtypeStruct((b, t, d), F32),
              jax.ShapeDtypeStruct((b, t, LANES), F32)]
    return shapes, [tok(d), tok(d), tok(LANES)]


def _post0_kernel(ol_ref, gm_ref, x_ref, gate_ref, wuv_ref, woa_ref, wob_ref,
                  nf_ref, shift_ref, scale_ref, wrh_ref, wrl_ref, rb_ref,
                  x1_ref, h2_ref, route_ref):
    parts = [_dot(ol_ref[0, hd], wuv_ref[hd]).astype(BF16) for hd in range(MLA_HEADS)]
    attn = jnp.concatenate(parts, axis=-1)
    y = _dot(attn, woa_ref[...]) + _dot(gm_ref[0], wob_ref[...])
    x1 = x_ref[0] + gate_ref[0] * y
    _moe_pre(x1, nf_ref, shift_ref, scale_ref, wrh_ref, wrl_ref, rb_ref, x1_ref, h2_ref, route_ref)


def _post0_call(ol, gm, x, gate, wuv, woa, wob, nf, shift2, scale2, wrh, wrl, rb, *, tm):
    b, t, d = x.shape
    tok = lambda width: pl.BlockSpec((1, tm, width), lambda i, j: (i, j, 0))
    mod = pl.BlockSpec((1, 1, d), lambda i, j: (i, 0, 0))
    shapes, specs = _moe_pre_outs(b, t, d, tm)
    return pl.pallas_call(
        _post0_kernel,
        grid=(b, t // tm),
        in_specs=[pl.BlockSpec((1, MLA_HEADS, tm, KV_LORA), lambda i, j: (i, 0, j, 0)),
                  tok(GM_WIDTH), tok(d), mod,
                  _const_spec(wuv.shape), _const_spec(woa.shape), _const_spec(wob.shape)]
                 + _moe_pre_specs(d, tm),
        out_specs=specs,
        out_shape=shapes,
        compiler_params=_cparams(("arbitrary", "arbitrary")),
        name="mla_gmlp_post",
    )(ol, gm, x, gate, wuv, woa, wob, nf, shift2, scale2, wrh, wrl, rb)


def _moe_kernel(be_ref, nu_ref, idx_ref, idxn_ref, h_hbm, wg_ref, wu_ref, wd_ref, y_ref,
                xbuf, sem, wgb, wub, wdb):
    i = pl.program_id(0)
    nu = nu_ref[0]
    slot = i % 2

    def row_copy(tok, r, s):
        return pltpu.make_async_copy(h_hbm.at[pl.ds(tok, 1)], xbuf.at[s, pl.ds(r, 1)], sem.at[s])

    def issue(iref, s):
        def body(r, c):
            row_copy(iref[0, 0, r], r, s).start()
            return c
        lax.fori_loop(0, MOE_ROWS, body, 0)

    @pl.when(i == 0)
    def _():
        issue(idx_ref, 0)

    @pl.when(i + 1 < nu)
    def _():
        issue(idxn_ref, 1 - slot)

    @pl.when(i < nu)
    def _():
        def wbody(r, c):
            row_copy(0, r, slot).wait()
            return c
        lax.fori_loop(0, MOE_ROWS, wbody, 0)

        prev = be_ref[jnp.maximum(i - 1, 0)]

        @pl.when((i == 0) | (be_ref[i] != prev))
        def _():
            wgb[...] = wg_ref[...].astype(BF16)
            wub[...] = wu_ref[...].astype(BF16)
            wdb[...] = wd_ref[...].astype(BF16)

        x = xbuf[slot].astype(BF16)
        g = _dot(x, wgb[...])
        u = _dot(x, wub[...])
        hid = (_silu(g) * u).astype(BF16)
        y_ref[...] = _dot(hid, wdb[...])

    @pl.when(i >= nu)
    def _():
        y_ref[...] = jnp.zeros(y_ref.shape, F32)


def _moe_call(block_e, n_used, slot_tok, h_all, w_gate, w_up, w_down, layer):
    n_blocks = block_e.shape[0]
    _, _, d, f = w_gate.shape
    idx3 = slot_tok.reshape(n_blocks, 1, MOE_ROWS)
    grid_spec = pltpu.PrefetchScalarGridSpec(
        num_scalar_prefetch=2,
        grid=(n_blocks,),
        in_specs=[
            pl.BlockSpec((1, 1, MOE_ROWS), lambda i, be, nu: (i, 0, 0), memory_space=pltpu.SMEM),
            pl.BlockSpec((1, 1, MOE_ROWS), lambda i, be, nu: (jnp.minimum(i + 1, n_blocks - 1), 0, 0),
                         memory_space=pltpu.SMEM),
            pl.BlockSpec(memory_space=pl.ANY),
            pl.BlockSpec((None, None, d, f), lambda i, be, nu: (layer, be[i], 0, 0)),
            pl.BlockSpec((None, None, d, f), lambda i, be, nu: (layer, be[i], 0, 0)),
            pl.BlockSpec((None, None, f, d), lambda i, be, nu: (layer, be[i], 0, 0)),
        ],
        out_specs=pl.BlockSpec((MOE_ROWS, d), lambda i, be, nu: (i, 0)),
        scratch_shapes=[pltpu.VMEM((2, MOE_ROWS, d), F32), pltpu.SemaphoreType.DMA((2,)),
                        pltpu.VMEM((d, f), BF16), pltpu.VMEM((d, f), BF16), pltpu.VMEM((f, d), BF16)],
    )
    return pl.pallas_call(
        _moe_kernel,
        grid_spec=grid_spec,
        out_shape=jax.ShapeDtypeStruct((n_blocks * MOE_ROWS, d), F32),
        compiler_params=_cparams(("arbitrary",)),
        name="moe_experts",
    )(block_e, n_used, idx3, idx3, h_all, w_gate, w_up, w_down)


def _moe_plan(route_all):
    n = route_all.shape[0]
    a = n * TOP_K
    flat_e = route_all[:, :TOP_K].astype(jnp.int32).reshape(a)
    onehot = (flat_e[:, None] == jnp.arange(N_EXPERTS, dtype=jnp.int32)[None, :]).astype(jnp.int32)
    csum = jnp.cumsum(onehot, axis=0)
    rank = jnp.take_along_axis(csum, flat_e[:, None], axis=1)[:, 0] - 1
    counts = csum[-1]
    padded = (counts + MOE_ROWS - 1) // MOE_ROWS * MOE_ROWS
    pad_end = jnp.cumsum(padded)
    pad_start = pad_end - padded
    dest = (pad_start[flat_e] + rank).astype(jnp.int32)
    n_blocks = -(-a // MOE_ROWS) + N_EXPERTS
    slot_tok = jnp.zeros((n_blocks * MOE_ROWS,), jnp.int32).at[dest].set(jnp.arange(a, dtype=jnp.int32) // TOP_K)
    block_e = jnp.minimum(jnp.searchsorted(pad_end, jnp.arange(n_blocks, dtype=jnp.int32) * MOE_ROWS, side='right'),
                          N_EXPERTS - 1).astype(jnp.int32)
    n_used = (pad_end[-1] // MOE_ROWS).astype(jnp.int32).reshape(1)
    return block_e, n_used, slot_tok, dest.reshape(n, TOP_K)


def _combine_kernel(ia_ref, ib_ref, ian_ref, ibn_ref, y_hbm, x_ref, gate_ref, route_ref, g_ref, *rest,
                    tm, nt, final):
    if final:
        out_ref, bufa, bufb, sem = rest
    else:
        shift_ref, scale_ref, x2_ref, h_ref, bufa, bufb, sem = rest
    i = pl.program_id(0)
    n = pl.num_programs(0)
    slot = i % 2

    def copies(ia, ib, r, s):
        return (pltpu.make_async_copy(y_hbm.at[pl.ds(ia, 1)], bufa.at[s, pl.ds(r, 1)], sem.at[s]),
                pltpu.make_async_copy(y_hbm.at[pl.ds(ib, 1)], bufb.at[s, pl.ds(r, 1)], sem.at[s]))

    def issue(ar, br, s):
        def body(r, c):
            ca, cb = copies(ar[0, 0, r], br[0, 0, r], r, s)
            ca.start()
            cb.start()
            return c
        lax.fori_loop(0, tm, body, 0)

    @pl.when(i == 0)
    def _():
        issue(ia_ref, ib_ref, 0)

    @pl.when(i + 1 < n)
    def _():
        issue(ian_ref, ibn_ref, 1 - slot)

    def wbody(r, c):
        ca, cb = copies(0, 0, r, slot)
        ca.wait()
        cb.wait()
        return c
    lax.fori_loop(0, tm, wbody, 0)

    route = route_ref[0]
    w0 = route[:, 2:3]
    w1 = route[:, 3:4]
    y = w0 * bufa[slot] + w1 * bufb[slot]
    x2 = x_ref[0] + gate_ref[0] * y
    if final:
        out_ref[0] = _rms(x2, g_ref[...])
    else:
        x2_ref[0] = x2
        h_ref[0] = (_rms(x2, g_ref[...]) * (1.0 + scale_ref[0]) + shift_ref[0]).astype(BF16)


def _combine_call(slots, yb, x, gate, route, g, shift=None, scale=None, *, tm, final):
    b, t, d = x.shape
    nt = t // tm
    n_steps = b * nt
    ia = slots[:, 0].reshape(n_steps, 1, tm)
    ib = slots[:, 1].reshape(n_steps, 1, tm)
    cur = pl.BlockSpec((1, 1, tm), lambda i: (i, 0, 0), memory_space=pltpu.SMEM)
    nxt = pl.BlockSpec((1, 1, tm), lambda i: (jnp.minimum(i + 1, n_steps - 1), 0, 0), memory_space=pltpu.SMEM)
    tok = lambda width: pl.BlockSpec((1, tm, width), lambda i: (i // nt, i % nt, 0))
    mod = pl.BlockSpec((1, 1, d), lambda i: (i // nt, 0, 0))
    in_specs = [cur, cur, nxt, nxt, pl.BlockSpec(memory_space=pl.ANY), tok(d), mod, tok(LANES),
                pl.BlockSpec((1, d), lambda i: (0, 0))]
    args = [ia, ib, ia, ib, yb, x, gate, route, g]
    if final:
        out_shape = jax.ShapeDtypeStruct((b, t, d), F32)
        out_specs = tok(d)
    else:
        in_specs += [mod, mod]
        args += [shift, scale]
        out_shape = [jax.ShapeDtypeStruct((b, t, d), F32), jax.ShapeDtypeStruct((b, t, d), BF16)]
        out_specs = [tok(d), tok(d)]
    return pl.pallas_call(
        functools.partial(_combine_kernel, tm=tm, nt=nt, final=final),
        grid=(n_steps,),
        in_specs=in_specs,
        out_specs=out_specs,
        out_shape=out_shape,
        scratch_shapes=[pltpu.VMEM((2, tm, d), F32), pltpu.VMEM((2, tm, d), F32),
                        pltpu.SemaphoreType.DMA((2,))],
        compiler_params=_cparams(("arbitrary",)),
        name="moe_combine_final" if final else "moe_combine",
    )(*args)


def _proj_kernel(h_ref, w_ref, o_ref):
    o_ref[...] = _dot(h_ref[...], w_ref[...]).astype(BF16)


def _proj_call(h, w, *, tm, tn):
    n, d = h.shape
    width = w.shape[1]
    return pl.pallas_call(
        _proj_kernel,
        grid=(width // tn, n // tm),
        in_specs=[pl.BlockSpec((tm, d), lambda c, j: (j, 0)),
                  pl.BlockSpec((d, tn), lambda c, j: (0, c))],
        out_specs=pl.BlockSpec((tm, tn), lambda c, j: (j, c)),
        out_shape=jax.ShapeDtypeStruct((n, width), BF16),
        compiler_params=_cparams(("arbitrary", "arbitrary")),
        name="gla_proj",
    )(h, w)


def _gate_kernel(h_ref, w1_ref, w2_ref, b_ref, o_ref):
    g_low = _dot(h_ref[...], w1_ref[...]).astype(BF16)
    z = _dot(g_low, w2_ref[...]) + b_ref[...]
    log_sig = jnp.minimum(z, 0.0) - jnp.log1p(jnp.exp(-jnp.abs(z)))
    o_ref[...] = log_sig * (1.0 / GLA_TAU)


def _gate_call(h, w1, w2, bg, *, tm):
    n, d = h.shape
    width = w2.shape[1]
    return pl.pallas_call(
        _gate_kernel,
        grid=(n // tm,),
        in_specs=[pl.BlockSpec((tm, d), lambda j: (j, 0)), _const_spec(w1.shape), _const_spec(w2.shape),
                  _const_spec(bg.shape)],
        out_specs=pl.BlockSpec((tm, width), lambda j: (j, 0)),
        out_shape=jax.ShapeDtypeStruct((n, width), F32),
        compiler_params=_cparams(("arbitrary",)),
        name="gla_gate",
    )(h, w1, w2, bg)


def _gla_intra(qf, kf, cum, c, dk):
    heads = qf.shape[1] // dk
    ri = lax.broadcasted_iota(jnp.int32, (c, c), 0)
    ci = lax.broadcasted_iota(jnp.int32, (c, c), 1)
    row = lax.broadcasted_iota(jnp.int32, (c, 1), 0)
    att = [jnp.zeros((c, c), F32) for _ in range(heads)]
    hs = c // 2
    while hs >= SUBLANES:
        pieces = []
        for blk in range(c // (2 * hs)):
            p = blk * 2 * hs + hs
            pieces.append(jnp.broadcast_to(cum[p - 1:p, :], (2 * hs, cum.shape[1])))
        ref = pieces[0] if len(pieces) == 1 else jnp.concatenate(pieces, axis=0)
        e = jnp.exp(-jnp.abs(cum - ref))
        is_q = ((row // hs) % 2) == 1
        qs = jnp.where(is_q, qf * e, 0.0).astype(BF16)
        ks = jnp.where(is_q, 0.0, kf * e).astype(BF16)
        mask = (((ri // hs) % 2) == 1) & (((ci // hs) % 2) == 0) & ((ri // (2 * hs)) == (ci // (2 * hs)))
        for hd in range(heads):
            sl = slice(hd * dk, (hd + 1) * dk)
            att[hd] = att[hd] + jnp.where(mask, _dot_nt(qs[:, sl], ks[:, sl]), 0.0)
        hs //= 2
    sub = lax.broadcasted_iota(jnp.int32, (SUBLANES, 1), 0)
    lane = lax.broadcasted_iota(jnp.int32, (SUBLANES, c), 1)
    slabs = [[] for _ in range(heads)]
    for blk in range(c // SUBLANES):
        rs = slice(blk * SUBLANES, (blk + 1) * SUBLANES)
        cb, qb, kb = cum[rs], qf[rs], kf[rs]
        slab = [jnp.zeros((SUBLANES, c), F32) for _ in range(heads)]
        for j in range(SUBLANES):
            dec = jnp.exp(jnp.minimum(cb - cb[j:j + 1], 0.0))
            prod = qb * kb[j:j + 1] * dec
            for hd in range(heads):
                col = jnp.sum(prod[:, hd * dk:(hd + 1) * dk], axis=-1, keepdims=True)
                hit = (lane == blk * SUBLANES + j) & (sub >= j)
                slab[hd] = slab[hd] + jnp.where(hit, col, 0.0)
        for hd in range(heads):
            slabs[hd].append(slab[hd])
    return [att[hd] + jnp.concatenate(slabs[hd], axis=0) for hd in range(heads)]


def _gla_kernel(*refs, c, nc, dk, dv, has_s0):
    if has_s0:
        q_ref, k_ref, v_ref, la_ref, tri_ref, hn_ref, s0_ref, o_ref, sout_ref, st_ref = refs
    else:
        q_ref, k_ref, v_ref, la_ref, tri_ref, hn_ref, o_ref, sout_ref, st_ref = refs
    j = pl.program_id(1)

    @pl.when(j == 0)
    def _():
        if has_s0:
            st_ref[...] = s0_ref[0]
        else:
            st_ref[...] = jnp.zeros(st_ref.shape, F32)

    tri = tri_ref[...]

    def chunk(ci, carry):
        r0 = pl.multiple_of(ci * c, c)
        rs = pl.ds(r0, c)
        g = la_ref[rs, :]
        g1 = g.astype(BF16)
        rem = g - g1.astype(F32)
        g2 = rem.astype(BF16)
        g3 = (rem - g2.astype(F32)).astype(BF16)
        cum = _dot(tri, g1) + _dot(tri, g2) + _dot(tri, g3)
        last = cum[c - 1:c, :]
        qf = q_ref[rs, :].astype(F32)
        kf = k_ref[rs, :].astype(F32)
        e_cum = jnp.exp(cum)
        q_in = (qf * e_cum).astype(BF16)
        k_dec = (kf * jnp.exp(last - cum)).astype(BF16)
        e_last = jnp.exp(last)
        att = _gla_intra(qf, kf, cum, c, dk)
        for hd in range(GLA_HEADS):
            ks = slice(hd * dk, (hd + 1) * dk)
            vs = slice(hd * dv, (hd + 1) * dv)
            vh = v_ref[rs, vs]
            st = st_ref[hd]
            o = _dot_nt(q_in[:, ks], st.astype(BF16)) + _dot(att[hd].astype(BF16), vh)
            o_ref[rs, vs] = _rms(o, hn_ref[...]).astype(BF16)
            st_ref[hd] = st * e_last[:, ks] + _dot_tn(vh, k_dec[:, ks])
        return carry

    lax.fori_loop(0, nc, chunk, 0)

    @pl.when(j == pl.num_programs(1) - 1)
    def _():
        sout_ref[0] = st_ref[...]


def _gla_call(p, la, tri, hn, s0t, *, b, t, c, nc, dk, dv):
    tt = c * nc
    nt = t // tt
    qk_w = GLA_HEADS * dk
    v_w = GLA_HEADS * dv
    rowblk = lambda i, j: i * nt + j
    in_specs = [
        pl.BlockSpec((tt, qk_w), lambda i, j: (rowblk(i, j), 0)),
        pl.BlockSpec((tt, qk_w), lambda i, j: (rowblk(i, j), 1)),
        pl.BlockSpec((tt, v_w), lambda i, j: (rowblk(i, j), 1)),
        pl.BlockSpec((tt, qk_w), lambda i, j: (rowblk(i, j), 0)),
        _const_spec(tri.shape), _const_spec(hn.shape),
    ]
    args = [p, p, p, la, tri, hn]
    if s0t is not None:
        in_specs.append(pl.BlockSpec((1, GLA_HEADS, dv, dk), lambda i, j: (i, 0, 0, 0)))
        args.append(s0t)
    return pl.pallas_call(
        functools.partial(_gla_kernel, c=c, nc=nc, dk=dk, dv=dv, has_s0=s0t is not None),
        grid=(b, nt),
        in_specs=in_specs,
        out_specs=[pl.BlockSpec((tt, v_w), lambda i, j: (rowblk(i, j), 0)),
                   pl.BlockSpec((1, GLA_HEADS, dv, dk), lambda i, j: (i, 0, 0, 0))],
        out_shape=[jax.ShapeDtypeStruct((b * t, v_w), BF16),
                   jax.ShapeDtypeStruct((b, GLA_HEADS, dv, dk), F32)],
        scratch_shapes=[pltpu.VMEM((GLA_HEADS, dv, dk), F32)],
        compiler_params=_cparams(("arbitrary", "arbitrary")),
        name="gla_scan",
    )(*args)


def _post1_kernel(on_ref, r_ref, x_ref, gate_ref, wo_ref,
                  nf_ref, shift_ref, scale_ref, wrh_ref, wrl_ref, rb_ref,
                  x1_ref, h2_ref, route_ref):
    gated = (on_ref[0].astype(F32) * _silu(r_ref[0].astype(F32))).astype(BF16)
    x1 = x_ref[0] + gate_ref[0] * _dot(gated, wo_ref[...])
    _moe_pre(x1, nf_ref, shift_ref, scale_ref, wrh_ref, wrl_ref, rb_ref, x1_ref, h2_ref, route_ref)


def _post1_call(on, p, x, gate, wo, nf, shift2, scale2, wrh, wrl, rb, *, tm):
    b, t, d = x.shape
    v_w = on.shape[-1]
    tok = lambda width: pl.BlockSpec((1, tm, width), lambda i, j: (i, j, 0))
    mod = pl.BlockSpec((1, 1, d), lambda i, j: (i, 0, 0))
    shapes, specs = _moe_pre_outs(b, t, d, tm)
    return pl.pallas_call(
        _post1_kernel,
        grid=(b, t // tm),
        in_specs=[tok(v_w), pl.BlockSpec((1, tm, v_w), lambda i, j: (i, j, 2)), tok(d), mod,
                  _const_spec(wo.shape)] + _moe_pre_specs(d, tm),
        out_specs=specs,
        out_shape=shapes,
        compiler_params=_cparams(("arbitrary", "arbitrary")),
        name="gla_post",
    )(on.reshape(b, t, v_w), p.reshape(b, t, p.shape[-1]), x, gate, wo, nf, shift2, scale2, wrh, wrl, rb)


def _rope_table(pos0, t):
    half = ROPE_DIM // 2
    inv = ROPE_THETA ** (-jnp.arange(half, dtype=F32) / half)
    ang = (pos0 + jnp.arange(t, dtype=jnp.int32)).astype(F32)[:, None] * inv[None, :]
    cos, sin = jnp.cos(ang), jnp.sin(ang)
    return jnp.concatenate([cos, cos, sin, sin], axis=-1)


def _rot_cols(w):
    half = ROPE_DIM // 2
    return jnp.concatenate([-w[..., half:], w[..., :half]], axis=-1)


def _split_hi_lo(w):
    hi = w.astype(BF16)
    return hi, (w - hi.astype(F32)).astype(BF16)


def kernel(x_prompt, x_sample, cache_ckv, cache_krope, state_gla, c_prompt, c_sample, ada_w, ada_b, norm_mix, norm_ffn, mla_w_in, mla_q_norm, mla_w_uq, mla_kv_norm, mla_w_uk, mla_w_uv, gm_ln_g, gm_ln_b, gm_ws, gm_bs, mix_a_w_out, gla_w_in, gla_w_gate2, gla_b_gate, gla_norm, gla_w_out, moe_wg, moe_bg, moe_we, moe_be, moe_w_gate, moe_w_up, moe_w_down, final_norm):
    d = x_prompt.shape[-1]
    bp, tp, _ = x_prompt.shape
    bs, ts, _ = x_sample.shape
    past = cache_ckv.shape[2]
    assert past % CHUNK == 0 and ts <= CHUNK and tp % 256 == 0
    dk = state_gla.shape[3]
    dv = state_gla.shape[4]
    row = lambda a: a.reshape(1, -1)

    n_seq = bp + bs
    r_pad = -(-n_seq // SUBLANES) * SUBLANES
    c_all = jnp.concatenate([c_prompt, c_sample, jnp.zeros((r_pad - n_seq, d), F32)], axis=0)
    mod = _ada_call(c_all, ada_w, ada_b)

    def mods(layer, sub, lo, n):
        m = mod[layer, sub, lo:lo + n]
        return m[:, None, :d], m[:, None, d:2 * d], m[:, None, 2 * d:]

    groups = [
        dict(x=x_prompt, lo=0, n=bp, t=tp, pos0=0, tm=256),
        dict(x=x_sample, lo=bp, n=bs, t=ts, pos0=past, tm=ts),
    ]

    w_in0 = mla_w_in[0]
    o_kr = Q_LORA + KV_LORA
    win = jnp.concatenate([w_in0[:, :o_kr + ROPE_DIM], _rot_cols(w_in0[:, o_kr:o_kr + ROPE_DIM]),
                           w_in0[:, o_kr + ROPE_DIM:]], axis=1).astype(BF16)
    wuq0 = mla_w_uq[0].reshape(Q_LORA, MLA_HEADS, NOPE_DIM + ROPE_DIM)
    wq_rope = wuq0[:, :, NOPE_DIM:]
    wuq = jnp.concatenate([wuq0[:, :, :NOPE_DIM].reshape(Q_LORA, -1),
                           jnp.concatenate([wq_rope, _rot_cols(wq_rope)], axis=-1).reshape(Q_LORA, -1)],
                          axis=1).astype(BF16)
    wuk = jnp.transpose(mla_w_uk[0], (1, 2, 0)).astype(BF16)
    wuv = jnp.transpose(mla_w_uv[0], (1, 0, 2)).astype(BF16)
    n_attn = MLA_HEADS * V_HEAD
    woa = mix_a_w_out[0][:n_attn].astype(BF16)
    wob = mix_a_w_out[0][n_attn:].astype(BF16)
    gw = gla_w_in[0]
    qk_w = GLA_HEADS * dk
    v_w = GLA_HEADS * dv
    gw_main = jnp.concatenate([gw[:, :qk_w] * (dk ** -0.5), gw[:, qk_w:2 * qk_w + 2 * v_w]], axis=1).astype(BF16)
    gw1 = jnp.pad(gw[:, 2 * qk_w + 2 * v_w:], ((0, 0), (0, LANES - GLA_GATE_RANK))).astype(BF16)
    gw2 = jnp.pad(gla_w_gate2[0], ((0, LANES - GLA_GATE_RANK), (0, 0))).astype(BF16)
    gwo = gla_w_out[0].astype(BF16)

    def router_w(layer):
        w = jnp.concatenate([moe_wg[layer], moe_we[layer]], axis=1)
        w = jnp.pad(w, ((0, 0), (0, LANES - w.shape[1])))
        bias = jnp.pad(jnp.concatenate([moe_bg[layer], moe_be[layer]]), (0, LANES - N_GROUPS - N_EXPERTS))
        return _split_hi_lo(w) + (row(bias),)

    def run_moe(layer, outs):
        h_all = jnp.concatenate([o[1].reshape(-1, d) for o in outs], axis=0)
        route_all = jnp.concatenate([o[2].reshape(-1, LANES) for o in outs], axis=0)
        block_e, n_used, slot_tok, dest = _moe_plan(route_all)
        yb = _moe_call(block_e, n_used, slot_tok, h_all, moe_w_gate, moe_w_up, moe_w_down, layer)
        n0 = outs[0][0].shape[0] * outs[0][0].shape[1]
        return yb, [dest[:n0], dest[n0:]]

    wrh0, wrl0, rb0 = router_w(0)
    outs0, new_ckv, new_krope, new_gmv = [], [], [], None
    for gi, g in enumerate(groups):
        shift, scale, gate = mods(0, 0, g['lo'], g['n'])
        shift2, scale2, _ = mods(0, 1, g['lo'], g['n'])
        cl = min(GM_CHUNK, g['t'])
        cs = _rope_table(g['pos0'], g['t'])
        res = _pre0_call(g['x'], shift, scale, row(norm_mix[0]), win, row(mla_q_norm[0]), wuq,
                         row(mla_kv_norm[0]), wuk, cs, row(gm_ln_g[0]), row(gm_ln_b[0]),
                         gm_ws[0][:, :cl, :cl], gm_bs[0].T[:cl], tm=g['tm'], want_v=(gi == 1))
        qp, kp, ckv, krope, gm = res[:5]
        new_ckv.append(ckv)
        new_krope.append(krope)
        if gi == 1:
            new_gmv = res[5]
            kpast = jnp.concatenate([cache_ckv[0].astype(BF16), cache_krope[0].astype(BF16),
                                     jnp.zeros((bs, past, ROPE_DIM), BF16)], axis=-1)
            n_keys = past + g['t']
            pad = -(-n_keys // ATT_TK) * ATT_TK - n_keys
            kp = jnp.concatenate([kpast, kp, jnp.zeros((bs, pad, QK_WIDTH), BF16)], axis=1)
        else:
            n_keys = g['t']
        ol = _flash_call(qp, kp, tq=g['tm'], q_pos0=g['pos0'], n_keys=n_keys)
        outs0.append(_post0_call(ol, gm, g['x'], gate, wuv, woa, wob, row(norm_ffn[0]), shift2, scale2,
                                 wrh0, wrl0, rb0, tm=g['tm']))
    yb0, slots0 = run_moe(0, outs0)

    wrh1, wrl1, rb1 = router_w(1)
    outs1, new_gla = [], []
    for gi, g in enumerate(groups):
        _, _, gate2 = mods(0, 1, g['lo'], g['n'])
        shift, scale, gate = mods(1, 0, g['lo'], g['n'])
        shift2, scale2, _ = mods(1, 1, g['lo'], g['n'])
        x1, _, route = outs0[gi]
        x2, h1 = _combine_call(slots0[gi], yb0, x1, gate2, route, row(norm_mix[1]), shift, scale,
                               tm=g['tm'], final=False)
        n_tok = g['n'] * g['t']
        h1f = h1.reshape(n_tok, d)
        tmm = min(512, n_tok)
        p = _proj_call(h1f, gw_main, tm=tmm, tn=1024)
        la = _gate_call(h1f, gw1, gw2, row(gla_b_gate[0]), tm=tmm)
        c = min(CHUNK, g['t'])
        nc = min(4, g['t'] // c)
        tri = jnp.tril(jnp.ones((c, c), F32)).astype(BF16)
        s0t = None if gi == 0 else jnp.swapaxes(state_gla[0], -1, -2)
        on, st = _gla_call(p, la, tri, row(gla_norm[0]), s0t, b=g['n'], t=g['t'], c=c, nc=nc, dk=dk, dv=dv)
        new_gla.append(jnp.swapaxes(st, -1, -2)[None])
        outs1.append(_post1_call(on, p, x2, gate, gwo, row(norm_ffn[1]), shift2, scale2,
                                 wrh1, wrl1, rb1, tm=g['tm']))
    yb1, slots1 = run_moe(1, outs1)

    ys = []
    for gi, g in enumerate(groups):
        _, _, gate2 = mods(1, 1, g['lo'], g['n'])
        x3, _, route = outs1[gi]
        ys.append(_combine_call(slots1[gi], yb1, x3, gate2, route, row(final_norm), tm=g['tm'], final=True))

    return (ys[0], ys[1], new_ckv[0][None], new_krope[0][None], new_ckv[1][None], new_krope[1][None],
            new_gmv[None], new_gla[0], new_gla[1])
```

```python
import functools
import math

import jax
import jax.numpy as jnp
from jax import lax
from jax.experimental import pallas as pl
from jax.experimental.pallas import tpu as pltpu

F32 = jnp.float32
BF16 = jnp.bfloat16

CHUNK = 64
EPS = 1e-6
MLA_HEADS = 8
Q_LORA = 512
KV_LORA = 256
NOPE_DIM = 128
ROPE_DIM = 64
V_HEAD = 128
ROPE_THETA = 10000.0
MLA_SCALE = (NOPE_DIM + ROPE_DIM) ** -0.5
QK_WIDTH = KV_LORA + 2 * ROPE_DIM
GM_GROUPS = 8
GM_GROUP_DIM = 128
GM_WIDTH = GM_GROUPS * GM_GROUP_DIM
GM_CHUNK = 128
GLA_HEADS = 4
GLA_GATE_RANK = 16
GLA_TAU = 16.0
N_GROUPS = 8
EXPERTS_PER_GROUP = 8
N_EXPERTS = N_GROUPS * EXPERTS_PER_GROUP
TOP_K = 2
LOG2E = 1.4426950408889634

LANES = 128
SUBLANES = 8
MOE_ROWS = 256
ATT_TK = 256
ATT_RB = 256
DMA_UNROLL = 8
VMEM_LIMIT = 56 * 1024 * 1024


def _cparams(sem):
    return pltpu.CompilerParams(dimension_semantics=sem, vmem_limit_bytes=VMEM_LIMIT)


def _const_spec(shape):
    nd = len(shape)
    return pl.BlockSpec(shape, lambda *_: (0,) * nd, pipeline_mode=pl.Buffered(1))


def _dot(a, b):
    return jnp.dot(a, b, preferred_element_type=F32)


def _dot_nt(a, b):
    return lax.dot_general(a, b, (((1,), (1,)), ((), ())), preferred_element_type=F32)


def _dot_tn(a, b):
    return lax.dot_general(a, b, (((0,), (0,)), ((), ())), preferred_element_type=F32)


def _rms(x, g):
    return x * lax.rsqrt(jnp.mean(x * x, axis=-1, keepdims=True) + EPS) * g


def _silu(x):
    return x * (1.0 / (1.0 + jnp.exp(-x)))


def _ada_kernel(c_ref, w_ref, b_ref, o_ref):
    s = _silu(c_ref[...]).astype(BF16)
    o_ref[0, 0] = _dot(s, w_ref[0, 0].astype(BF16)) + b_ref[0, 0]


def _ada_call(c_all, ada_w, ada_b):
    depth, _, d, d3 = ada_w.shape
    r = c_all.shape[0]
    tn = 1024
    return pl.pallas_call(
        _ada_kernel,
        grid=(depth, 2, d3 // tn),
        in_specs=[
            pl.BlockSpec((r, d), lambda l, s, j: (0, 0)),
            pl.BlockSpec((1, 1, d, tn), lambda l, s, j: (l, s, 0, j)),
            pl.BlockSpec((1, 1, 1, tn), lambda l, s, j: (l, s, 0, j)),
        ],
        out_specs=pl.BlockSpec((1, 1, r, tn), lambda l, s, j: (l, s, 0, j)),
        out_shape=jax.ShapeDtypeStruct((depth, 2, r, d3), F32),
        compiler_params=_cparams(("arbitrary",) * 3),
        name="adaln",
    )(c_all, ada_w, ada_b.reshape(depth, 2, 1, d3))


def _pre0_kernel(x_ref, shift_ref, scale_ref, gmix_ref, win_ref, qn_ref, wuq_ref, kvn_ref, wuk_ref,
                 cs_ref, lng_ref, lnb_ref, ws_ref, bst_ref,
                 q_ref, kp_ref, ckv_ref, kr_ref, gm_ref, *rest, cl):
    tm = x_ref.shape[1]
    x = x_ref[0]
    h = (_rms(x, gmix_ref[...]) * (1.0 + scale_ref[0]) + shift_ref[0]).astype(BF16)
    y = _dot(h, win_ref[...])
    cs = cs_ref[...]

    cq = _rms(y[:, :Q_LORA], qn_ref[...]).astype(BF16)
    qf = _dot(cq, wuq_ref[...])
    qscale = MLA_SCALE * LOG2E
    nope_w = MLA_HEADS * NOPE_DIM
    for hd in range(MLA_HEADS):
        qn = qf[:, hd * NOPE_DIM:(hd + 1) * NOPE_DIM].astype(BF16)
        ql = _dot(qn, wuk_ref[hd])
        t = qf[:, nope_w + hd * LANES:nope_w + (hd + 1) * LANES] * cs
        qr = t + pltpu.roll(t, ROPE_DIM, 1)
        q_ref[0, hd, :, :KV_LORA] = (ql * qscale).astype(BF16)
        q_ref[0, hd, :, KV_LORA:] = (qr * qscale).astype(BF16)

    o0 = Q_LORA
    ckv = _rms(y[:, o0:o0 + KV_LORA], kvn_ref[...])
    ckv_ref[0] = ckv
    o1 = o0 + KV_LORA
    tk = y[:, o1:o1 + LANES] * cs
    kk = tk + pltpu.roll(tk, ROPE_DIM, 1)
    kr_ref[0] = kk[:, :ROPE_DIM]
    lane = lax.broadcasted_iota(jnp.int32, (tm, LANES), 1)
    kp_ref[0, :, :KV_LORA] = ckv.astype(BF16)
    kp_ref[0, :, KV_LORA:] = jnp.where(lane < ROPE_DIM, kk, 0.0).astype(BF16)

    o2 = o1 + LANES
    u = jax.nn.gelu(y[:, o2:o2 + GM_WIDTH])
    v = jax.nn.gelu(y[:, o2 + GM_WIDTH:o2 + 2 * GM_WIDTH])
    vc = v - jnp.mean(v, axis=-1, keepdims=True)
    v = vc * lax.rsqrt(jnp.mean(vc * vc, axis=-1, keepdims=True) + EPS) * lng_ref[...] + lnb_ref[...]
    if rest:
        rest[0][0] = v
    vb = v.astype(BF16)
    ri = lax.broadcasted_iota(jnp.int32, (cl, cl), 0)
    ci = lax.broadcasted_iota(jnp.int32, (cl, cl), 1)
    for g in range(GM_GROUPS):
        w = jnp.where(ci <= ri, ws_ref[g], 0.0).astype(BF16)
        bias = bst_ref[:, g:g + 1]
        gs = slice(g * GM_GROUP_DIM, (g + 1) * GM_GROUP_DIM)
        for c in range(tm // cl):
            rs = slice(c * cl, (c + 1) * cl)
            mixed = _dot(w, vb[rs, gs]) + bias
            gm_ref[0, rs, gs] = (u[rs, gs] * mixed).astype(BF16)


def _pre0_call(x, shift, scale, gmix, win, qn, wuq, kvn, wuk, cs, lng, lnb, ws, bst, *, tm, want_v):
    b, t, d = x.shape
    cl = ws.shape[1]
    nt = t // tm
    tok = lambda width: pl.BlockSpec((1, tm, width), lambda i, j: (i, j, 0))
    mod = pl.BlockSpec((1, 1, d), lambda i, j: (i, 0, 0))
    out_shape = [
        jax.ShapeDtypeStruct((b, MLA_HEADS, t, QK_WIDTH), BF16),
        jax.ShapeDtypeStruct((b, t, QK_WIDTH), BF16),
        jax.ShapeDtypeStruct((b, t, KV_LORA), F32),
        jax.ShapeDtypeStruct((b, t, ROPE_DIM), F32),
        jax.ShapeDtypeStruct((b, t, GM_WIDTH), BF16),
    ]
    out_specs = [
        pl.BlockSpec((1, MLA_HEADS, tm, QK_WIDTH), lambda i, j: (i, 0, j, 0)),
        tok(QK_WIDTH), tok(KV_LORA), tok(ROPE_DIM), tok(GM_WIDTH),
    ]
    if want_v:
        out_shape.append(jax.ShapeDtypeStruct((b, t, GM_WIDTH), F32))
        out_specs.append(tok(GM_WIDTH))
    return pl.pallas_call(
        functools.partial(_pre0_kernel, cl=cl),
        grid=(b, nt),
        in_specs=[
            tok(d), mod, mod, _const_spec(gmix.shape), _const_spec(win.shape), _const_spec(qn.shape),
            _const_spec(wuq.shape), _const_spec(kvn.shape), _const_spec(wuk.shape),
            pl.BlockSpec((tm, LANES), lambda i, j: (j, 0)),
            _const_spec(lng.shape), _const_spec(lnb.shape), _const_spec(ws.shape), _const_spec(bst.shape),
        ],
        out_specs=out_specs,
        out_shape=out_shape,
        compiler_params=_cparams(("arbitrary", "arbitrary")),
        name="mla_gmlp_pre",
    )(x, shift, scale, gmix, win, qn, wuq, kvn, wuk, cs, lng, lnb, ws, bst)


def _flash_kernel(q_ref, kv_ref, o_ref, s_ref, m_ref, l_ref, acc_ref, *, tq, q_pos0, n_keys):
    qi = pl.program_id(1)
    rows = MLA_HEADS * tq
    rb = min(rows, ATT_RB)
    nsub = rows // rb
    nrep = ATT_TK // LANES
    vrep = KV_LORA // LANES
    m_ref[...] = jnp.full((rows, LANES), -jnp.inf, F32)
    l_ref[...] = jnp.zeros((rows, LANES), F32)
    acc_ref[...] = jnp.zeros((rows, KV_LORA), F32)
    q_start = q_pos0 + qi * tq
    n_full = q_start // ATT_TK

    def k_tile(j):
        return kv_ref[0, pl.ds(pl.multiple_of(j * ATT_TK, ATT_TK), ATT_TK), :]

    def scores(j, slot):
        s_ref[slot] = _dot_nt(q_ref[0].reshape(rows, QK_WIDTH), k_tile(j))

    def process(j, slot, masked):
        v = k_tile(j)[:, :KV_LORA]
        ps, alphas = [], []
        for sb in range(nsub):
            rs = slice(sb * rb, (sb + 1) * rb)
            s = s_ref[slot, rs, :]
            if masked:
                row = lax.broadcasted_iota(jnp.int32, (rb, ATT_TK), 0) + sb * rb
                col = lax.broadcasted_iota(jnp.int32, (rb, ATT_TK), 1)
                qpos = q_start + (row & (tq - 1))
                kpos = j * ATT_TK + col
                vis = (kpos < n_keys) & ((kpos // CHUNK) <= (qpos // CHUNK))
                s = jnp.where(vis, s, -jnp.inf)
            m_prev = m_ref[rs]
            m_new = jnp.maximum(m_prev, jnp.max(s, axis=-1, keepdims=True))
            alpha = jnp.exp2(m_prev - m_new)
            p = jnp.exp2(s - jnp.concatenate([m_new] * nrep, axis=1))
            l_ref[rs] = alpha * l_ref[rs] + jnp.sum(p, axis=-1, keepdims=True)
            m_ref[rs] = m_new
            ps.append(p.astype(BF16))
            alphas.append(alpha)
        pv = _dot(ps[0] if nsub == 1 else jnp.concatenate(ps, axis=0), v)
        for sb in range(nsub):
            rs = slice(sb * rb, (sb + 1) * rb)
            acc_ref[rs] = jnp.concatenate([alphas[sb]] * vrep, axis=1) * acc_ref[rs] + pv[rs]

    scores(0, 0)

    def pair(i, c):
        j = 2 * i
        scores(j + 1, 1)
        process(j, 0, False)
        scores(j + 2, 0)
        process(j + 1, 1, False)
        return c

    lax.fori_loop(0, n_full // 2, pair, 0)

    @pl.when(n_full % 2 == 1)
    def _():
        scores(n_full, 1)
        process(n_full - 1, 0, False)
        process(n_full, 1, True)

    @pl.when(n_full % 2 == 0)
    def _():
        process(n_full, 0, True)

    linv = 1.0 / l_ref[...]
    o = acc_ref[...] * jnp.concatenate([linv] * vrep, axis=1)
    o_ref[0] = o.reshape(MLA_HEADS, tq, KV_LORA).astype(BF16)


def _flash_call(qp, kp, *, tq, q_pos0, n_keys):
    b, hh, t, _ = qp.shape
    tk_total = kp.shape[1]
    assert tq & (tq - 1) == 0 and tk_total % ATT_TK == 0
    for qi in range(t // tq):
        q_start = q_pos0 + qi * tq
        chunk_end = min(((q_start + tq - 1) // CHUNK + 1) * CHUNK, n_keys)
        assert q_start % CHUNK == 0 and -(-chunk_end // ATT_TK) == q_start // ATT_TK + 1 <= tk_total // ATT_TK
    rows = hh * tq
    return pl.pallas_call(
        functools.partial(_flash_kernel, tq=tq, q_pos0=q_pos0, n_keys=n_keys),
        grid=(b, t // tq),
        in_specs=[
            pl.BlockSpec((1, hh, tq, QK_WIDTH), lambda i, j: (i, 0, j, 0)),
            pl.BlockSpec((1, tk_total, QK_WIDTH), lambda i, j: (i, 0, 0), pipeline_mode=pl.Buffered(1)),
        ],
        out_specs=pl.BlockSpec((1, hh, tq, KV_LORA), lambda i, j: (i, 0, j, 0)),
        out_shape=jax.ShapeDtypeStruct((b, hh, t, KV_LORA), BF16),
        scratch_shapes=[pltpu.VMEM((2, rows, ATT_TK), F32), pltpu.VMEM((rows, LANES), F32),
                        pltpu.VMEM((rows, LANES), F32), pltpu.VMEM((rows, KV_LORA), F32)],
        compiler_params=_cparams(("arbitrary", "arbitrary")),
        name="mla_flash",
    )(qp, kp)


def _attn_cached_kernel(q_ref, cc_ref, ck_ref, kn_ref, o_ref):
    hh, tq, _ = q_ref.shape[1:]
    q = q_ref[0].reshape(hh * tq, QK_WIDTH)
    ckv_past = cc_ref[0, 0].astype(BF16)
    kn = kn_ref[0]
    s_past = (_dot_nt(q[:, :KV_LORA], ckv_past)
              + _dot_nt(q[:, KV_LORA:KV_LORA + ROPE_DIM], ck_ref[0, 0].astype(BF16)))
    s_new = _dot_nt(q, kn)
    m = jnp.maximum(jnp.max(s_past, axis=-1, keepdims=True), jnp.max(s_new, axis=-1, keepdims=True))
    p_past = jnp.exp2(s_past - m)
    p_new = jnp.exp2(s_new - m)
    l = jnp.sum(p_past, axis=-1, keepdims=True) + jnp.sum(p_new, axis=-1, keepdims=True)
    o = _dot(p_past.astype(BF16), ckv_past) + _dot(p_new.astype(BF16), kn[:, :KV_LORA])
    o_ref[0] = (o * (1.0 / l)).reshape(hh, tq, KV_LORA).astype(BF16)


def _attn_cached_call(qp, cache_ckv, cache_krope, kp_new):
    b, hh, tq, _ = qp.shape
    past = cache_ckv.shape[2]
    return pl.pallas_call(
        _attn_cached_kernel,
        grid=(b,),
        in_specs=[
            pl.BlockSpec((1, hh, tq, QK_WIDTH), lambda i: (i, 0, 0, 0)),
            pl.BlockSpec((1, 1, past, KV_LORA), lambda i: (0, i, 0, 0)),
            pl.BlockSpec((1, 1, past, ROPE_DIM), lambda i: (0, i, 0, 0)),
            pl.BlockSpec((1, tq, QK_WIDTH), lambda i: (i, 0, 0)),
        ],
        out_specs=pl.BlockSpec((1, hh, tq, KV_LORA), lambda i: (i, 0, 0, 0)),
        out_shape=jax.ShapeDtypeStruct((b, hh, tq, KV_LORA), BF16),
        compiler_params=_cparams(("arbitrary",)),
        name="mla_attn_cached",
    )(qp, cache_ckv, cache_krope, kp_new)


def _moe_pre(x1, nf_ref, shift_ref, scale_ref, wrh_ref, wrl_ref, rb_ref, x1_ref, h2_ref, route_ref):
    tm = x1.shape[0]
    x1_ref[0] = x1
    h2 = _rms(x1, nf_ref[...]) * (1.0 + scale_ref[0]) + shift_ref[0]
    h_hi = h2.astype(BF16)
    h_lo = (h2 - h_hi.astype(F32)).astype(BF16)
    h2_ref[0] = h2
    logits = _dot(h_hi, wrh_ref[...]) + _dot(h_hi, wrl_ref[...]) + _dot(h_lo, wrh_ref[...]) + rb_ref[...]
    lane = lax.broadcasted_iota(jnp.int32, (tm, LANES), 1).astype(F32)
    neg = -jnp.inf
    is_g = lane < N_GROUPS
    gmax = jnp.max(jnp.where(is_g, logits, neg), axis=-1, keepdims=True)
    grp = jnp.min(jnp.where(is_g & (logits == gmax), lane, float(LANES)), axis=-1, keepdims=True)
    p_grp = 1.0 / jnp.sum(jnp.where(is_g, jnp.exp(logits - gmax), 0.0), axis=-1, keepdims=True)
    lo = N_GROUPS + EXPERTS_PER_GROUP * grp
    is_e = (lane >= lo) & (lane < lo + EXPERTS_PER_GROUP)
    el = jnp.where(is_e, logits, neg)
    v1 = jnp.max(el, axis=-1, keepdims=True)
    i1 = jnp.min(jnp.where(is_e & (logits == v1), lane, float(LANES)), axis=-1, keepdims=True)
    el2 = jnp.where(lane == i1, neg, el)
    v2 = jnp.max(el2, axis=-1, keepdims=True)
    i2 = jnp.min(jnp.where(is_e & (lane != i1) & (logits == v2), lane, float(LANES)), axis=-1, keepdims=True)
    e = jnp.exp(v2 - v1)
    w1 = p_grp / (1.0 + e)
    w2 = p_grp * e / (1.0 + e)
    route = jnp.where(lane == 0.0, i1 - N_GROUPS,
                      jnp.where(lane == 1.0, i2 - N_GROUPS,
                                jnp.where(lane == 2.0, w1, jnp.where(lane == 3.0, w2, 0.0))))
    route_ref[0] = route


def _moe_pre_specs(d, tm):
    mod = pl.BlockSpec((1, 1, d), lambda i, j: (i, 0, 0))
    return [_const_spec((1, d)), mod, mod, _const_spec((d, LANES)), _const_spec((d, LANES)),
            _const_spec((1, LANES))]


def _moe_pre_outs(b, t, d, tm):
    tok = lambda width: pl.BlockSpec((1, tm, width), lambda i, j: (i, j, 0))
    shapes = [jax.ShapeDtypeStruct((b, t, d), F32), jax.ShapeDtypeStruct((b, t, d), F32),
              jax.ShapeDtypeStruct((b, t, LANES), F32)]
    return shapes, [tok(d), tok(d), tok(LANES)]


def _post0_kernel(ol_ref, gm_ref, x_ref, gate_ref, wuv_ref, woa_ref, wob_ref,
                  nf_ref, shift_ref, scale_ref, wrh_ref, wrl_ref, rb_ref,
                  x1_ref, h2_ref, route_ref):
    parts = [_dot(ol_ref[0, hd], wuv_ref[hd]).astype(BF16) for hd in range(MLA_HEADS)]
    attn = jnp.concatenate(parts, axis=-1)
    y = _dot(attn, woa_ref[...]) + _dot(gm_ref[0], wob_ref[...])
    x1 = x_ref[0] + gate_ref[0] * y
    _moe_pre(x1, nf_ref, shift_ref, scale_ref, wrh_ref, wrl_ref, rb_ref, x1_ref, h2_ref, route_ref)


def _post0_call(ol, gm, x, gate, wuv, woa, wob, nf, shift2, scale2, wrh, wrl, rb, *, tm):
    b, t, d = x.shape
    tok = lambda width: pl.BlockSpec((1, tm, width), lambda i, j: (i, j, 0))
    mod = pl.BlockSpec((1, 1, d), lambda i, j: (i, 0, 0))
    shapes, specs = _moe_pre_outs(b, t, d, tm)
    return pl.pallas_call(
        _post0_kernel,
        grid=(b, t // tm),
        in_specs=[pl.BlockSpec((1, MLA_HEADS, tm, KV_LORA), lambda i, j: (i, 0, j, 0)),
                  tok(GM_WIDTH), tok(d), mod,
                  _const_spec(wuv.shape), _const_spec(woa.shape), _const_spec(wob.shape)]
                 + _moe_pre_specs(d, tm),
        out_specs=specs,
        out_shape=shapes,
        compiler_params=_cparams(("arbitrary", "arbitrary")),
        name="mla_gmlp_post",
    )(ol, gm, x, gate, wuv, woa, wob, nf, shift2, scale2, wrh, wrl, rb)


def _moe_kernel(be_ref, nu_ref, idx_ref, idxn_ref, h_hbm, wg_ref, wu_ref, wd_ref, y_ref,
                xbuf, sem, wgb, wub, wdb):
    i = pl.program_id(0)
    nu = nu_ref[0]
    slot = i % 2

    def row_copy(tok, r, s):
        return pltpu.make_async_copy(h_hbm.at[pl.ds(tok, 1)], xbuf.at[s, pl.ds(r, 1)], sem.at[s])

    def issue(iref, s):
        def body(r, c):
            row_copy(iref[0, 0, r], r, s).start()
            return c
        lax.fori_loop(0, MOE_ROWS, body, 0, unroll=DMA_UNROLL)

    @pl.when(i == 0)
    def _():
        issue(idx_ref, 0)

    @pl.when(i + 1 < nu)
    def _():
        issue(idxn_ref, 1 - slot)

    @pl.when(i < nu)
    def _():
        pltpu.make_async_copy(h_hbm.at[pl.ds(0, MOE_ROWS)], xbuf.at[slot], sem.at[slot]).wait()

        prev = be_ref[jnp.maximum(i - 1, 0)]

        @pl.when((i == 0) | (be_ref[i] != prev))
        def _():
            wgb[...] = wg_ref[...].astype(BF16)
            wub[...] = wu_ref[...].astype(BF16)
            wdb[...] = wd_ref[...].astype(BF16)

        x = xbuf[slot].astype(BF16)
        g = _dot(x, wgb[...])
        u = _dot(x, wub[...])
        hid = (_silu(g) * u).astype(BF16)
        y_ref[...] = _dot(hid, wdb[...])

    @pl.when(i >= nu)
    def _():
        y_ref[...] = jnp.zeros(y_ref.shape, F32)


def _moe_call(block_e, n_used, slot_tok, h_all, w_gate, w_up, w_down, layer):
    n_blocks = block_e.shape[0]
    _, _, d, f = w_gate.shape
    idx3 = slot_tok.reshape(n_blocks, 1, MOE_ROWS)
    grid_spec = pltpu.PrefetchScalarGridSpec(
        num_scalar_prefetch=2,
        grid=(n_blocks,),
        in_specs=[
            pl.BlockSpec((1, 1, MOE_ROWS), lambda i, be, nu: (i, 0, 0), memory_space=pltpu.SMEM),
            pl.BlockSpec((1, 1, MOE_ROWS), lambda i, be, nu: (jnp.minimum(i + 1, n_blocks - 1), 0, 0),
                         memory_space=pltpu.SMEM),
            pl.BlockSpec(memory_space=pl.ANY),
            pl.BlockSpec((None, None, d, f), lambda i, be, nu: (layer, be[i], 0, 0)),
            pl.BlockSpec((None, None, d, f), lambda i, be, nu: (layer, be[i], 0, 0)),
            pl.BlockSpec((None, None, f, d), lambda i, be, nu: (layer, be[i], 0, 0)),
        ],
        out_specs=pl.BlockSpec((MOE_ROWS, d), lambda i, be, nu: (i, 0)),
        scratch_shapes=[pltpu.VMEM((2, MOE_ROWS, d), F32), pltpu.SemaphoreType.DMA((2,)),
                        pltpu.VMEM((d, f), BF16), pltpu.VMEM((d, f), BF16), pltpu.VMEM((f, d), BF16)],
    )
    return pl.pallas_call(
        _moe_kernel,
        grid_spec=grid_spec,
        out_shape=jax.ShapeDtypeStruct((n_blocks * MOE_ROWS, d), F32),
        compiler_params=_cparams(("arbitrary",)),
        name="moe_experts",
    )(block_e, n_used, idx3, idx3, h_all, w_gate, w_up, w_down)


def _moe_plan(route_all):
    n = route_all.shape[0]
    a = n * TOP_K
    flat_e = route_all[:, :TOP_K].astype(jnp.int32).reshape(a)
    onehot = (flat_e[:, None] == jnp.arange(N_EXPERTS, dtype=jnp.int32)[None, :]).astype(jnp.int32)
    csum = jnp.cumsum(onehot, axis=0)
    rank = jnp.take_along_axis(csum, flat_e[:, None], axis=1)[:, 0] - 1
    counts = csum[-1]
    padded = (counts + MOE_ROWS - 1) // MOE_ROWS * MOE_ROWS
    pad_end = jnp.cumsum(padded)
    pad_start = pad_end - padded
    dest = (pad_start[flat_e] + rank).astype(jnp.int32)
    n_blocks = -(-a // MOE_ROWS) + N_EXPERTS
    slot_tok = jnp.zeros((n_blocks * MOE_ROWS,), jnp.int32).at[dest].set(jnp.arange(a, dtype=jnp.int32) // TOP_K)
    block_e = jnp.minimum(jnp.searchsorted(pad_end, jnp.arange(n_blocks, dtype=jnp.int32) * MOE_ROWS, side='right'),
                          N_EXPERTS - 1).astype(jnp.int32)
    n_used = (pad_end[-1] // MOE_ROWS).astype(jnp.int32).reshape(1)
    return block_e, n_used, slot_tok, dest.reshape(n, TOP_K)


def _combine_kernel(ia_ref, ib_ref, ian_ref, ibn_ref, y_hbm, x_ref, gate_ref, route_ref, g_ref, *rest,
                    tm, nt, final):
    if final:
        out_ref, bufa, bufb, sem = rest
    else:
        shift_ref, scale_ref, x2_ref, h_ref, bufa, bufb, sem = rest
    i = pl.program_id(0)
    n = pl.num_programs(0)
    slot = i % 2

    def copies(ia, ib, r, s):
        return (pltpu.make_async_copy(y_hbm.at[pl.ds(ia, 1)], bufa.at[s, pl.ds(r, 1)], sem.at[s]),
                pltpu.make_async_copy(y_hbm.at[pl.ds(ib, 1)], bufb.at[s, pl.ds(r, 1)], sem.at[s]))

    def issue(ar, br, s):
        def body(r, c):
            ca, cb = copies(ar[0, 0, r], br[0, 0, r], r, s)
            ca.start()
            cb.start()
            return c
        lax.fori_loop(0, tm, body, 0, unroll=DMA_UNROLL)

    @pl.when(i == 0)
    def _():
        issue(ia_ref, ib_ref, 0)

    @pl.when(i + 1 < n)
    def _():
        issue(ian_ref, ibn_ref, 1 - slot)

    pltpu.make_async_copy(y_hbm.at[pl.ds(0, tm)], bufa.at[slot], sem.at[slot]).wait()
    pltpu.make_async_copy(y_hbm.at[pl.ds(0, tm)], bufb.at[slot], sem.at[slot]).wait()

    route = route_ref[0]
    w0 = route[:, 2:3]
    w1 = route[:, 3:4]
    y = w0 * bufa[slot] + w1 * bufb[slot]
    x2 = x_ref[0] + gate_ref[0] * y
    if final:
        out_ref[0] = _rms(x2, g_ref[...])
    else:
        x2_ref[0] = x2
        h_ref[0] = (_rms(x2, g_ref[...]) * (1.0 + scale_ref[0]) + shift_ref[0]).astype(BF16)


def _combine_call(slots, yb, x, gate, route, g, shift=None, scale=None, *, tm, final):
    b, t, d = x.shape
    nt = t // tm
    n_steps = b * nt
    ia = slots[:, 0].reshape(n_steps, 1, tm)
    ib = slots[:, 1].reshape(n_steps, 1, tm)
    cur = pl.BlockSpec((1, 1, tm), lambda i: (i, 0, 0), memory_space=pltpu.SMEM)
    nxt = pl.BlockSpec((1, 1, tm), lambda i: (jnp.minimum(i + 1, n_steps - 1), 0, 0), memory_space=pltpu.SMEM)
    tok = lambda width: pl.BlockSpec((1, tm, width), lambda i: (i // nt, i % nt, 0))
    mod = pl.BlockSpec((1, 1, d), lambda i: (i // nt, 0, 0))
    in_specs = [cur, cur, nxt, nxt, pl.BlockSpec(memory_space=pl.ANY), tok(d), mod, tok(LANES),
                pl.BlockSpec((1, d), lambda i: (0, 0))]
    args = [ia, ib, ia, ib, yb, x, gate, route, g]
    if final:
        out_shape = jax.ShapeDtypeStruct((b, t, d), F32)
        out_specs = tok(d)
    else:
        in_specs += [mod, mod]
        args += [shift, scale]
        out_shape = [jax.ShapeDtypeStruct((b, t, d), F32), jax.ShapeDtypeStruct((b, t, d), BF16)]
        out_specs = [tok(d), tok(d)]
    return pl.pallas_call(
        functools.partial(_combine_kernel, tm=tm, nt=nt, final=final),
        grid=(n_steps,),
        in_specs=in_specs,
        out_specs=out_specs,
        out_shape=out_shape,
        scratch_shapes=[pltpu.VMEM((2, tm, d), F32), pltpu.VMEM((2, tm, d), F32),
                        pltpu.SemaphoreType.DMA((2,))],
        compiler_params=_cparams(("arbitrary",)),
        name="moe_combine_final" if final else "moe_combine",
    )(*args)


def _proj_kernel(h_ref, w_ref, o_ref):
    o_ref[...] = _dot(h_ref[...], w_ref[...]).astype(BF16)


def _proj_call(h, w, *, tm, tn):
    n, d = h.shape
    width = w.shape[1]
    return pl.pallas_call(
        _proj_kernel,
        grid=(width // tn, n // tm),
        in_specs=[pl.BlockSpec((tm, d), lambda c, j: (j, 0)),
                  pl.BlockSpec((d, tn), lambda c, j: (0, c))],
        out_specs=pl.BlockSpec((tm, tn), lambda c, j: (j, c)),
        out_shape=jax.ShapeDtypeStruct((n, width), BF16),
        compiler_params=_cparams(("arbitrary", "arbitrary")),
        name="gla_proj",
    )(h, w)


def _gate_kernel(h_ref, w1_ref, w2_ref, b_ref, o_ref):
    g_low = _dot(h_ref[...], w1_ref[...]).astype(BF16)
    z = _dot(g_low, w2_ref[...]) + b_ref[...]
    log_sig = jnp.minimum(z, 0.0) - jnp.log1p(jnp.exp(-jnp.abs(z)))
    o_ref[...] = log_sig * (1.0 / GLA_TAU)


def _gate_call(h, w1, w2, bg, *, tm):
    n, d = h.shape
    width = w2.shape[1]
    return pl.pallas_call(
        _gate_kernel,
        grid=(n // tm,),
        in_specs=[pl.BlockSpec((tm, d), lambda j: (j, 0)), _const_spec(w1.shape), _const_spec(w2.shape),
                  _const_spec(bg.shape)],
        out_specs=pl.BlockSpec((tm, width), lambda j: (j, 0)),
        out_shape=jax.ShapeDtypeStruct((n, width), F32),
        compiler_params=_cparams(("arbitrary",)),
        name="gla_gate",
    )(h, w1, w2, bg)


def _gla_intra(qf, kf, cum, c, dk):
    heads = qf.shape[1] // dk
    ri = lax.broadcasted_iota(jnp.int32, (c, c), 0)
    ci = lax.broadcasted_iota(jnp.int32, (c, c), 1)
    row = lax.broadcasted_iota(jnp.int32, (c, 1), 0)
    att = [jnp.zeros((c, c), F32) for _ in range(heads)]
    hs = c // 2
    while hs >= SUBLANES:
        pieces = []
        for blk in range(c // (2 * hs)):
            p = blk * 2 * hs + hs
            pieces.append(jnp.broadcast_to(cum[p - 1:p, :], (2 * hs, cum.shape[1])))
        ref = pieces[0] if len(pieces) == 1 else jnp.concatenate(pieces, axis=0)
        e = jnp.exp(-jnp.abs(cum - ref))
        is_q = ((row // hs) % 2) == 1
        qs = jnp.where(is_q, qf * e, 0.0).astype(BF16)
        ks = jnp.where(is_q, 0.0, kf * e).astype(BF16)
        mask = (((ri // hs) % 2) == 1) & (((ci // hs) % 2) == 0) & ((ri // (2 * hs)) == (ci // (2 * hs)))
        for hd in range(heads):
            sl = slice(hd * dk, (hd + 1) * dk)
            att[hd] = att[hd] + jnp.where(mask, _dot_nt(qs[:, sl], ks[:, sl]), 0.0)
        hs //= 2
    sub = lax.broadcasted_iota(jnp.int32, (SUBLANES, 1), 0)
    lane = lax.broadcasted_iota(jnp.int32, (SUBLANES, c), 1)
    slabs = [[] for _ in range(heads)]
    for blk in range(c // SUBLANES):
        rs = slice(blk * SUBLANES, (blk + 1) * SUBLANES)
        cb, qb, kb = cum[rs], qf[rs], kf[rs]
        slab = [jnp.zeros((SUBLANES, c), F32) for _ in range(heads)]
        for j in range(SUBLANES):
            dec = jnp.exp(jnp.minimum(cb - cb[j:j + 1], 0.0))
            prod = qb * kb[j:j + 1] * dec
            for hd in range(heads):
                col = jnp.sum(prod[:, hd * dk:(hd + 1) * dk], axis=-1, keepdims=True)
                hit = (lane == blk * SUBLANES + j) & (sub >= j)
                slab[hd] = slab[hd] + jnp.where(hit, col, 0.0)
        for hd in range(heads):
            slabs[hd].append(slab[hd])
    return [att[hd] + jnp.concatenate(slabs[hd], axis=0) for hd in range(heads)]


def _gla_kernel(*refs, c, nc, dk, dv, has_s0):
    if has_s0:
        q_ref, k_ref, v_ref, la_ref, tri_ref, hn_ref, s0_ref, o_ref, sout_ref, st_ref = refs
    else:
        q_ref, k_ref, v_ref, la_ref, tri_ref, hn_ref, o_ref, sout_ref, st_ref = refs
    j = pl.program_id(1)

    @pl.when(j == 0)
    def _():
        if has_s0:
            st_ref[...] = s0_ref[0]
        else:
            st_ref[...] = jnp.zeros(st_ref.shape, F32)

    tri = tri_ref[...]

    def chunk(ci, carry):
        r0 = pl.multiple_of(ci * c, c)
        rs = pl.ds(r0, c)
        g = la_ref[rs, :]
        g1 = g.astype(BF16)
        rem = g - g1.astype(F32)
        g2 = rem.astype(BF16)
        g3 = (rem - g2.astype(F32)).astype(BF16)
        cum = _dot(tri, g1) + _dot(tri, g2) + _dot(tri, g3)
        last = cum[c - 1:c, :]
        qf = q_ref[rs, :].astype(F32)
        kf = k_ref[rs, :].astype(F32)
        e_cum = jnp.exp(cum)
        q_in = (qf * e_cum).astype(BF16)
        k_dec = (kf * jnp.exp(last - cum)).astype(BF16)
        e_last = jnp.exp(last)
        att = _gla_intra(qf, kf, cum, c, dk)
        for hd in range(GLA_HEADS):
            ks = slice(hd * dk, (hd + 1) * dk)
            vs = slice(hd * dv, (hd + 1) * dv)
            vh = v_ref[rs, vs]
            st = st_ref[hd]
            o = _dot_nt(q_in[:, ks], st.astype(BF16)) + _dot(att[hd].astype(BF16), vh)
            o_ref[rs, vs] = _rms(o, hn_ref[...]).astype(BF16)
            st_ref[hd] = st * e_last[:, ks] + _dot_tn(vh, k_dec[:, ks])
        return carry

    lax.fori_loop(0, nc, chunk, 0)

    @pl.when(j == pl.num_programs(1) - 1)
    def _():
        sout_ref[0] = st_ref[...]


def _gla_call(p, la, tri, hn, s0t, *, b, t, c, nc, dk, dv):
    tt = c * nc
    nt = t // tt
    qk_w = GLA_HEADS * dk
    v_w = GLA_HEADS * dv
    rowblk = lambda i, j: i * nt + j
    in_specs = [
        pl.BlockSpec((tt, qk_w), lambda i, j: (rowblk(i, j), 0)),
        pl.BlockSpec((tt, qk_w), lambda i, j: (rowblk(i, j), 1)),
        pl.BlockSpec((tt, v_w), lambda i, j: (rowblk(i, j), 1)),
        pl.BlockSpec((tt, qk_w), lambda i, j: (rowblk(i, j), 0)),
        _const_spec(tri.shape), _const_spec(hn.shape),
    ]
    args = [p, p, p, la, tri, hn]
    if s0t is not None:
        in_specs.append(pl.BlockSpec((1, GLA_HEADS, dv, dk), lambda i, j: (i, 0, 0, 0)))
        args.append(s0t)
    return pl.pallas_call(
        functools.partial(_gla_kernel, c=c, nc=nc, dk=dk, dv=dv, has_s0=s0t is not None),
        grid=(b, nt),
        in_specs=in_specs,
        out_specs=[pl.BlockSpec((tt, v_w), lambda i, j: (rowblk(i, j), 0)),
                   pl.BlockSpec((1, GLA_HEADS, dv, dk), lambda i, j: (i, 0, 0, 0))],
        out_shape=[jax.ShapeDtypeStruct((b * t, v_w), BF16),
                   jax.ShapeDtypeStruct((b, GLA_HEADS, dv, dk), F32)],
        scratch_shapes=[pltpu.VMEM((GLA_HEADS, dv, dk), F32)],
        compiler_params=_cparams(("arbitrary", "arbitrary")),
        name="gla_scan",
    )(*args)


def _post1_kernel(on_ref, r_ref, x_ref, gate_ref, wo_ref,
                  nf_ref, shift_ref, scale_ref, wrh_ref, wrl_ref, rb_ref,
                  x1_ref, h2_ref, route_ref):
    gated = (on_ref[0].astype(F32) * _silu(r_ref[0].astype(F32))).astype(BF16)
    x1 = x_ref[0] + gate_ref[0] * _dot(gated, wo_ref[...])
    _moe_pre(x1, nf_ref, shift_ref, scale_ref, wrh_ref, wrl_ref, rb_ref, x1_ref, h2_ref, route_ref)


def _post1_call(on, p, x, gate, wo, nf, shift2, scale2, wrh, wrl, rb, *, tm):
    b, t, d = x.shape
    v_w = on.shape[-1]
    tok = lambda width: pl.BlockSpec((1, tm, width), lambda i, j: (i, j, 0))
    mod = pl.BlockSpec((1, 1, d), lambda i, j: (i, 0, 0))
    shapes, specs = _moe_pre_outs(b, t, d, tm)
    return pl.pallas_call(
        _post1_kernel,
        grid=(b, t // tm),
        in_specs=[tok(v_w), pl.BlockSpec((1, tm, v_w), lambda i, j: (i, j, 2)), tok(d), mod,
                  _const_spec(wo.shape)] + _moe_pre_specs(d, tm),
        out_specs=specs,
        out_shape=shapes,
        compiler_params=_cparams(("arbitrary", "arbitrary")),
        name="gla_post",
    )(on.reshape(b, t, v_w), p.reshape(b, t, p.shape[-1]), x, gate, wo, nf, shift2, scale2, wrh, wrl, rb)


def _rope_table(pos0, t):
    half = ROPE_DIM // 2
    inv = ROPE_THETA ** (-jnp.arange(half, dtype=F32) / half)
    ang = (pos0 + jnp.arange(t, dtype=jnp.int32)).astype(F32)[:, None] * inv[None, :]
    cos, sin = jnp.cos(ang), jnp.sin(ang)
    return jnp.concatenate([cos, cos, sin, sin], axis=-1)


def _rot_cols(w):
    half = ROPE_DIM // 2
    return jnp.concatenate([-w[..., half:], w[..., :half]], axis=-1)


def _split_hi_lo(w):
    hi = w.astype(BF16)
    return hi, (w - hi.astype(F32)).astype(BF16)


def kernel(x_prompt, x_sample, cache_ckv, cache_krope, state_gla, c_prompt, c_sample, ada_w, ada_b, norm_mix, norm_ffn, mla_w_in, mla_q_norm, mla_w_uq, mla_kv_norm, mla_w_uk, mla_w_uv, gm_ln_g, gm_ln_b, gm_ws, gm_bs, mix_a_w_out, gla_w_in, gla_w_gate2, gla_b_gate, gla_norm, gla_w_out, moe_wg, moe_bg, moe_we, moe_be, moe_w_gate, moe_w_up, moe_w_down, final_norm):
    d = x_prompt.shape[-1]
    bp, tp, _ = x_prompt.shape
    bs, ts, _ = x_sample.shape
    past = cache_ckv.shape[2]
    assert past % CHUNK == 0 and ts <= CHUNK and tp % 256 == 0
    dk = state_gla.shape[3]
    dv = state_gla.shape[4]
    row = lambda a: a.reshape(1, -1)

    n_seq = bp + bs
    r_pad = -(-n_seq // SUBLANES) * SUBLANES
    c_all = jnp.concatenate([c_prompt, c_sample, jnp.zeros((r_pad - n_seq, d), F32)], axis=0)
    mod = _ada_call(c_all, ada_w, ada_b)

    def mods(layer, sub, lo, n):
        m = mod[layer, sub, lo:lo + n]
        return m[:, None, :d], m[:, None, d:2 * d], m[:, None, 2 * d:]

    groups = [
        dict(x=x_prompt, lo=0, n=bp, t=tp, pos0=0, tm=256),
        dict(x=x_sample, lo=bp, n=bs, t=ts, pos0=past, tm=ts),
    ]

    w_in0 = mla_w_in[0]
    o_kr = Q_LORA + KV_LORA
    win = jnp.concatenate([w_in0[:, :o_kr + ROPE_DIM], _rot_cols(w_in0[:, o_kr:o_kr + ROPE_DIM]),
                           w_in0[:, o_kr + ROPE_DIM:]], axis=1).astype(BF16)
    wuq0 = mla_w_uq[0].reshape(Q_LORA, MLA_HEADS, NOPE_DIM + ROPE_DIM)
    wq_rope = wuq0[:, :, NOPE_DIM:]
    wuq = jnp.concatenate([wuq0[:, :, :NOPE_DIM].reshape(Q_LORA, -1),
                           jnp.concatenate([wq_rope, _rot_cols(wq_rope)], axis=-1).reshape(Q_LORA, -1)],
                          axis=1).astype(BF16)
    wuk = jnp.transpose(mla_w_uk[0], (1, 2, 0)).astype(BF16)
    wuv = jnp.transpose(mla_w_uv[0], (1, 0, 2)).astype(BF16)
    n_attn = MLA_HEADS * V_HEAD
    woa = mix_a_w_out[0][:n_attn].astype(BF16)
    wob = mix_a_w_out[0][n_attn:].astype(BF16)
    gw = gla_w_in[0]
    qk_w = GLA_HEADS * dk
    v_w = GLA_HEADS * dv
    gw_main = jnp.concatenate([gw[:, :qk_w] * (dk ** -0.5), gw[:, qk_w:2 * qk_w + 2 * v_w]], axis=1).astype(BF16)
    gw1 = jnp.pad(gw[:, 2 * qk_w + 2 * v_w:], ((0, 0), (0, LANES - GLA_GATE_RANK))).astype(BF16)
    gw2 = jnp.pad(gla_w_gate2[0], ((0, LANES - GLA_GATE_RANK), (0, 0))).astype(BF16)
    gwo = gla_w_out[0].astype(BF16)

    def router_w(layer):
        w = jnp.concatenate([moe_wg[layer], moe_we[layer]], axis=1)
        w = jnp.pad(w, ((0, 0), (0, LANES - w.shape[1])))
        bias = jnp.pad(jnp.concatenate([moe_bg[layer], moe_be[layer]]), (0, LANES - N_GROUPS - N_EXPERTS))
        return _split_hi_lo(w) + (row(bias),)

    def run_moe(layer, outs):
        h_all = jnp.concatenate([o[1].reshape(-1, d) for o in outs], axis=0)
        route_all = jnp.concatenate([o[2].reshape(-1, LANES) for o in outs], axis=0)
        block_e, n_used, slot_tok, dest = _moe_plan(route_all)
        yb = _moe_call(block_e, n_used, slot_tok, h_all, moe_w_gate, moe_w_up, moe_w_down, layer)
        n0 = outs[0][0].shape[0] * outs[0][0].shape[1]
        return yb, [dest[:n0], dest[n0:]]

    wrh0, wrl0, rb0 = router_w(0)
    outs0, new_ckv, new_krope, new_gmv = [], [], [], None
    for gi, g in enumerate(groups):
        shift, scale, gate = mods(0, 0, g['lo'], g['n'])
        shift2, scale2, _ = mods(0, 1, g['lo'], g['n'])
        cl = min(GM_CHUNK, g['t'])
        cs = _rope_table(g['pos0'], g['t'])
        res = _pre0_call(g['x'], shift, scale, row(norm_mix[0]), win, row(mla_q_norm[0]), wuq,
                         row(mla_kv_norm[0]), wuk, cs, row(gm_ln_g[0]), row(gm_ln_b[0]),
                         gm_ws[0][:, :cl, :cl], gm_bs[0].T[:cl], tm=g['tm'], want_v=(gi == 1))
        qp, kp, ckv, krope, gm = res[:5]
        new_ckv.append(ckv)
        new_krope.append(krope)
        if gi == 1:
            new_gmv = res[5]
            ol = _attn_cached_call(qp, cache_ckv, cache_krope, kp)
        else:
            ol = _flash_call(qp, kp, tq=g['tm'], q_pos0=0, n_keys=g['t'])
        outs0.append(_post0_call(ol, gm, g['x'], gate, wuv, woa, wob, row(norm_ffn[0]), shift2, scale2,
                                 wrh0, wrl0, rb0, tm=g['tm']))
    yb0, slots0 = run_moe(0, outs0)

    wrh1, wrl1, rb1 = router_w(1)
    outs1, new_gla = [], []
    for gi, g in enumerate(groups):
        _, _, gate2 = mods(0, 1, g['lo'], g['n'])
        shift, scale, gate = mods(1, 0, g['lo'], g['n'])
        shift2, scale2, _ = mods(1, 1, g['lo'], g['n'])
        x1, _, route = outs0[gi]
        x2, h1 = _combine_call(slots0[gi], yb0, x1, gate2, route, row(norm_mix[1]), shift, scale,
                               tm=g['tm'], final=False)
        n_tok = g['n'] * g['t']
        h1f = h1.reshape(n_tok, d)
        tmm = min(512, n_tok)
        p = _proj_call(h1f, gw_main, tm=tmm, tn=1024)
        la = _gate_call(h1f, gw1, gw2, row(gla_b_gate[0]), tm=tmm)
        c = min(CHUNK, g['t'])
        nc = min(4, g['t'] // c)
        tri = jnp.tril(jnp.ones((c, c), F32)).astype(BF16)
        s0t = None if gi == 0 else jnp.swapaxes(state_gla[0], -1, -2)
        on, st = _gla_call(p, la, tri, row(gla_norm[0]), s0t, b=g['n'], t=g['t'], c=c, nc=nc, dk=dk, dv=dv)
        new_gla.append(jnp.swapaxes(st, -1, -2)[None])
        outs1.append(_post1_call(on, p, x2, gate, gwo, row(norm_ffn[1]), shift2, scale2,
                                 wrh1, wrl1, rb1, tm=g['tm']))
    yb1, slots1 = run_moe(1, outs1)

    ys = []
    for gi, g in enumerate(groups):
        _, _, gate2 = mods(1, 1, g['lo'], g['n'])
        x3, _, route = outs1[gi]
        ys.append(_combine_call(slots1[gi], yb1, x3, gate2, route, row(final_norm), tm=g['tm'], final=True))

    return (ys[0], ys[1], new_ckv[0][None], new_krope[0][None], new_ckv[1][None], new_krope[1][None],
            new_gmv[None], new_gla[0], new_gla[1])
```
